```python
import math
import jax
import jax.numpy as jnp
from jax import lax
import numpy as np

D_MODEL = 1024
BATCH = 2
SEQ = 8192
DEPTH = 4

CHUNK = 64
SSM_WIDTH = D_MODEL // 2
SSM_GROUP = 16
SSM_GROUPS = SSM_WIDTH // SSM_GROUP
SSM_STATE = 64
ATTN_WIDTH = D_MODEL // 2
N_HEADS = 8
HEAD_DIM = ATTN_WIDTH // N_HEADS
Q_BLOCK = 128
EPS = 1e-6
DT_MIN = 0.001
DT_MAX = 0.1
IN_SIZES = (SSM_WIDTH, SSM_WIDTH, ATTN_WIDTH, ATTN_WIDTH, ATTN_WIDTH, ATTN_WIDTH, D_MODEL, D_MODEL)
IN_COLS = sum(IN_SIZES)
IN_SPLITS = tuple(int(s) for s in np.cumsum(IN_SIZES)[:-1])

kernel_name = "hybrid_s5_stickbreaking_gated_block"


def rms_norm(x, g):
    xf = x.astype(jnp.float32)
    y = xf * lax.rsqrt(jnp.mean(xf * xf, axis=-1, keepdims=True) + EPS)
    return (y * g.astype(jnp.float32)).astype(x.dtype)


def s5_branch(u, a_re, a_im, log_dt, b_re, b_im, c_re, c_im, d_skip, w_glu, b_glu):
    bsz, seqlen, _ = u.shape
    f32 = jnp.float32
    ug = u.reshape(bsz, seqlen, SSM_GROUPS, SSM_GROUP).astype(f32)
    a_re = a_re.astype(f32)
    a_im = a_im.astype(f32)
    dt = jnp.exp(log_dt.astype(f32))[:, None]
    mag = jnp.exp(a_re * dt)
    abar_re = mag * jnp.cos(a_im * dt)
    abar_im = mag * jnp.sin(a_im * dt)
    nr = abar_re - 1.0
    ni = abar_im
    den = a_re * a_re + a_im * a_im
    f_re = (nr * a_re + ni * a_im) / den
    f_im = (ni * a_re - nr * a_im) / den
    b_re = b_re.astype(f32)
    b_im = b_im.astype(f32)
    bb_re = f_re[..., None] * b_re - f_im[..., None] * b_im
    bb_im = f_re[..., None] * b_im + f_im[..., None] * b_re
    bu_re = jnp.einsum('blgh,gph->blgp', ug, bb_re)
    bu_im = jnp.einsum('blgh,gph->blgp', ug, bb_im)
    ar_t = jnp.broadcast_to(abar_re, bu_re.shape)
    ai_t = jnp.broadcast_to(abar_im, bu_re.shape)

    def combine(e1, e2):
        a1r, a1i, b1r, b1i = e1
        a2r, a2i, b2r, b2i = e2
        return (a1r * a2r - a1i * a2i,
                a1r * a2i + a1i * a2r,
                a2r * b1r - a2i * b1i + b2r,
                a2r * b1i + a2i * b1r + b2i)

    _, _, xr, xi = lax.associative_scan(combine, (ar_t, ai_t, bu_re, bu_im), axis=1)
    y = (jnp.einsum('blgp,ghp->blgh', xr, c_re.astype(f32))
         - jnp.einsum('blgp,ghp->blgh', xi, c_im.astype(f32)))
    y = y + d_skip.astype(f32).reshape(SSM_GROUPS, SSM_GROUP) * ug
    y = jax.nn.gelu(y.reshape(bsz, seqlen, SSM_WIDTH)).astype(u.dtype)
    gl = y @ w_glu + b_glu
    ga, gb = jnp.split(gl, 2, axis=-1)
    return ga * jax.nn.sigmoid(gb)


def stick_breaking_attention(q, k, v):
    seqlen = q.shape[1]
    scale = HEAD_DIM ** -0.5
    outs = []
    for i in range(seqlen // Q_BLOCK):
        q0 = i * Q_BLOCK
        kend = q0 + Q_BLOCK
        qb = q[:, q0:kend]
        kb = k[:, :kend]
        vb = v[:, :kend]
        z = jnp.einsum('bqhd,bkhd->bhqk', qb, kb).astype(jnp.float32) * scale
        qpos = q0 + jnp.arange(Q_BLOCK)[:, None]
        kpos = jnp.arange(kend)[None, :]
        mask = kpos < qpos
        log_beta = jax.nn.log_sigmoid(z)
        log_1m = jnp.where(mask, log_beta - z, 0.0)
        after = lax.cumsum(log_1m, axis=3, reverse=True) - log_1m
        w = jnp.where(mask, jnp.exp(log_beta + after), 0.0)
        outs.append(jnp.einsum('bhqk,bkhd->bqhd', w.astype(v.dtype), vb))
    return jnp.concatenate(outs, axis=1)


def setup_inputs(seed: int = 0) -> dict:
    key = jax.random.key(seed)
    ks = jax.random.split(key, 20)
    f32 = jnp.float32
    nrm = lambda k, shape, s: jax.random.normal(k, shape, f32) * s
    x = jax.random.normal(ks[0], (BATCH, SEQ, D_MODEL), f32)
    pre_norm_g = 1.0 + nrm(ks[1], (DEPTH, D_MODEL), 0.02)
    post_norm_g = 1.0 + nrm(ks[2], (DEPTH, D_MODEL), 0.02)
    w_in = nrm(ks[3], (DEPTH, D_MODEL, IN_COLS), D_MODEL ** -0.5)
    ssm_a_re = -0.5 + nrm(ks[4], (DEPTH, SSM_GROUPS, SSM_STATE), 0.01)
    ssm_a_im = (math.pi * jnp.arange(SSM_STATE, dtype=f32))[None, None, :] + nrm(ks[5], (DEPTH, SSM_GROUPS, SSM_STATE), 0.01)
    ssm_log_dt = jax.random.uniform(ks[6], (DEPTH, SSM_GROUPS), f32, math.log(DT_MIN), math.log(DT_MAX))
    ssm_b_re = nrm(ks[7], (DEPTH, SSM_GROUPS, SSM_STATE, SSM_GROUP), (2 * SSM_GROUP) ** -0.5)
    ssm_b_im = nrm(ks[8], (DEPTH, SSM_GROUPS, SSM_STATE, SSM_GROUP), (2 * SSM_GROUP) ** -0.5)
    ssm_c_re = nrm(ks[9], (DEPTH, SSM_GROUPS, SSM_GROUP, SSM_STATE), (2 * SSM_STATE) ** -0.5)
    ssm_c_im = nrm(ks[10], (DEPTH, SSM_GROUPS, SSM_GROUP, SSM_STATE), (2 * SSM_STATE) ** -0.5)
    ssm_d = nrm(ks[11], (DEPTH, SSM_WIDTH), 1.0)
    w_glu = nrm(ks[12], (DEPTH, SSM_WIDTH, 2 * SSM_WIDTH), SSM_WIDTH ** -0.5)
    b_glu = nrm(ks[13], (DEPTH, 2 * SSM_WIDTH), 0.01)
    w_branch_ssm = nrm(ks[14], (DEPTH, SSM_WIDTH, D_MODEL), SSM_WIDTH ** -0.5)
    w_branch_attn = nrm(ks[15], (DEPTH, ATTN_WIDTH, D_MODEL), ATTN_WIDTH ** -0.5)
    w_out = nrm(ks[16], (DEPTH, D_MODEL, D_MODEL), D_MODEL ** -0.5)
    return {"x": x, "pre_norm_g": pre_norm_g, "post_norm_g": post_norm_g, "w_in": w_in,
            "ssm_a_re": ssm_a_re, "ssm_a_im": ssm_a_im, "ssm_log_dt": ssm_log_dt,
            "ssm_b_re": ssm_b_re, "ssm_b_im": ssm_b_im, "ssm_c_re": ssm_c_re, "ssm_c_im": ssm_c_im,
            "ssm_d": ssm_d, "w_glu": w_glu, "b_glu": b_glu,
            "w_branch_ssm": w_branch_ssm, "w_branch_attn": w_branch_attn, "w_out": w_out}


def reference(x, pre_norm_g, post_norm_g, w_in, ssm_a_re, ssm_a_im, ssm_log_dt,
              ssm_b_re, ssm_b_im, ssm_c_re, ssm_c_im, ssm_d, w_glu, b_glu,
              w_branch_ssm, w_branch_attn, w_out):
    bsz, seqlen, _ = x.shape
    for l in range(DEPTH):
        h = rms_norm(x, pre_norm_g[l])
        proj = h @ w_in[l]
        u, z_ssm, q, k, v, z_attn, g_ssm, g_attn = jnp.split(proj, IN_SPLITS, axis=-1)
        y_s = s5_branch(u, ssm_a_re[l], ssm_a_im[l], ssm_log_dt[l], ssm_b_re[l], ssm_b_im[l],
                        ssm_c_re[l], ssm_c_im[l], ssm_d[l], w_glu[l], b_glu[l])
        y_s = y_s * jax.nn.silu(z_ssm)
        hs = (bsz, seqlen, N_HEADS, HEAD_DIM)
        y_a = stick_breaking_attention(q.reshape(hs), k.reshape(hs), v.reshape(hs))
        y_a = y_a.reshape(bsz, seqlen, ATTN_WIDTH) * jax.nn.silu(z_attn)
        merged = (jax.nn.sigmoid(g_ssm) * (y_s @ w_branch_ssm[l])
                  + jax.nn.sigmoid(g_attn) * (y_a @ w_branch_attn[l]))
        out = merged @ w_out[l]
        x = x + rms_norm(out, post_norm_g[l])
    return x
```

```python
import functools
import math

import jax
import jax.numpy as jnp
from jax import lax
from jax.experimental import pallas as pl
from jax.experimental.pallas import tpu as pltpu

F32 = jnp.float32
BF16 = jnp.bfloat16

D_MODEL = 1024
SSM_WIDTH = 512
SSM_GROUP = 16
SSM_GROUPS = 32
SSM_STATE = 64
ATTN_WIDTH = 512
HEAD_DIM = 64
EPS = 1e-6
IN_COLS = 5120

LANES = 128
SUBLANES = 8
N_SLABS = SSM_WIDTH // LANES
SLAB_STATE = (LANES // SSM_GROUP) * SSM_STATE
STATE_CHUNKS = SLAB_STATE // LANES
N_STATE_VREGS = 4

VMEM_LIMIT = 56 * 1024 * 1024

LOG2E = math.log2(math.e)
Q_SCALE = HEAD_DIM ** -0.5 * LOG2E
EXIT_LOG2 = 150.0


def _sigmoid(x):
    return 1.0 / (1.0 + jnp.exp(-x))


def _silu(x):
    return x * _sigmoid(x)


def _gelu_tanh(x):
    return 0.5 * x * (1.0 + jnp.tanh(math.sqrt(2.0 / math.pi) * (x + 0.044715 * (x * x * x))))


def _in_proj_kernel(x_ref, g_ref, w_ref, u_ref, zs_ref, q_ref, k_ref, v_ref, za_ref, gs_ref, ga_ref):
    x = x_ref[...]
    ms = jnp.mean(x * x, axis=-1, keepdims=True)
    h = (x * lax.rsqrt(ms + EPS) * g_ref[...]).astype(BF16)

    def proj(lo, hi):
        return jnp.dot(h, w_ref[:, lo:hi], preferred_element_type=F32)

    u_ref[...] = proj(0, 512).astype(BF16)
    zs_ref[...] = _silu(proj(512, 1024)).astype(BF16)
    q_ref[...] = (proj(1024, 1536) * Q_SCALE).astype(BF16)
    k_ref[...] = proj(1536, 2048).astype(BF16)
    v_ref[...] = proj(2048, 2560).astype(BF16)
    za_ref[...] = _silu(proj(2560, 3072)).astype(BF16)
    gs_ref[...] = _sigmoid(proj(3072, 4096)).astype(BF16)
    ga_ref[...] = _sigmoid(proj(4096, 5120)).astype(BF16)


def _in_proj(x, g, w, *, tm):
    t = x.shape[0]
    row = lambda n: pl.BlockSpec((tm, n), lambda i: (i, 0))
    full = lambda a: pl.BlockSpec(a.shape, lambda i: (0,) * a.ndim)
    widths = (512, 512, 512, 512, 512, 512, 1024, 1024)
    return pl.pallas_call(
        _in_proj_kernel,
        grid=(t // tm,),
        in_specs=[row(D_MODEL), full(g), full(w)],
        out_specs=[row(n) for n in widths],
        out_shape=[jax.ShapeDtypeStruct((t, n), BF16) for n in widths],
        compiler_params=pltpu.CompilerParams(
            dimension_semantics=("arbitrary",), vmem_limit_bytes=VMEM_LIMIT),
        name="in_proj",
    )(x, g, w)


def _ssm_disc_kernel(are_ref, aim_ref, ldt_ref, bre_ref, bim_ref, abr_ref, abi_ref, bbr_ref, bbi_ref):
    a_re = are_ref[0]
    a_im = aim_ref[0]
    dt = jnp.exp(ldt_ref[0])
    mag = jnp.exp(a_re * dt)
    abar_re = mag * jnp.cos(a_im * dt)
    abar_im = mag * jnp.sin(a_im * dt)
    nr = abar_re - 1.0
    ni = abar_im
    den = a_re * a_re + a_im * a_im
    f_re = (nr * a_re + ni * a_im) / den
    f_im = (ni * a_re - nr * a_im) / den
    abr_ref[0] = abar_re
    abi_ref[0] = abar_im
    for h in range(SSM_GROUP):
        b_re = bre_ref[0, h]
        b_im = bim_ref[0, h]
        bbr_ref[0, h] = f_re * b_re - f_im * b_im
        bbi_ref[0, h] = f_re * b_im + f_im * b_re


def _ssm_discretise(a_re, a_im, log_dt, b_re_t, b_im_t):
    depth = a_re.shape[0]
    spec = lambda a: pl.BlockSpec((1,) + a.shape[1:], lambda l: (l,) + (0,) * (a.ndim - 1))
    ins = (a_re, a_im, log_dt, b_re_t, b_im_t)
    outs = (a_re, a_im, b_re_t, b_im_t)
    return pl.pallas_call(
        _ssm_disc_kernel,
        grid=(depth,),
        in_specs=[spec(a) for a in ins],
        out_specs=[spec(a) for a in outs],
        out_shape=[jax.ShapeDtypeStruct(a.shape, F32) for a in outs],
        name="ssm_discretise",
    )(*ins)


def _block_diag_slabs(w):
    g, a, b = w.shape
    gl = g // N_SLABS
    w = w.reshape(N_SLABS, gl, a, b)
    eye = jnp.eye(gl, dtype=w.dtype)
    out = w[:, :, :, None, :] * eye[None, :, None, :, None]
    return out.reshape(N_SLABS, gl * a, gl * b)


def _ssm_kernel(u_ref, wb_ref, ar_ref, ai_ref, wc_ref, d_ref, y_ref, bu_scr, x_scr, st_scr, *, nb, lc):
    @pl.when(pl.program_id(0) == 0)
    def _():
        st_scr[...] = jnp.zeros_like(st_scr)

    nt = lc // SUBLANES

    for b in range(nb):
        for s in range(N_SLABS):
            bu = jnp.dot(u_ref[b, :, s * LANES:(s + 1) * LANES], wb_ref[s], preferred_element_type=F32)
            vp, base = s // 2, (s % 2) * STATE_CHUNKS
            for c in range(STATE_CHUNKS):
                j = base + c
                re = bu[:, c * LANES:(c + 1) * LANES]
                im = bu[:, SLAB_STATE + c * LANES:SLAB_STATE + (c + 1) * LANES]
                bu_scr[b, 2 * vp, :, j * SUBLANES:(j + 1) * SUBLANES, :] = re.reshape(nt, SUBLANES, LANES)
                bu_scr[b, 2 * vp + 1, :, j * SUBLANES:(j + 1) * SUBLANES, :] = im.reshape(nt, SUBLANES, LANES)

    a_r = [ar_ref[0], ar_ref[1]]
    a_i = [ai_ref[0], ai_ref[1]]

    def steps(i, carry):
        xs = list(carry)
        for r in range(SUBLANES):
            row = pl.multiple_of(i * (SUBLANES * SUBLANES) + r * SUBLANES, SUBLANES)
            for b in range(nb):
                for vp in range(2):
                    k = (b * 2 + vp) * 2
                    b_r = bu_scr[b, 2 * vp, i, pl.ds(r, SUBLANES, stride=SUBLANES), :]
                    b_i = bu_scr[b, 2 * vp + 1, i, pl.ds(r, SUBLANES, stride=SUBLANES), :]
                    x_r, x_i = xs[k], xs[k + 1]
                    n_r = a_r[vp] * x_r - a_i[vp] * x_i + b_r
                    n_i = a_r[vp] * x_i + a_i[vp] * x_r + b_i
                    x_scr[b, 2 * vp, pl.ds(row, SUBLANES), :] = n_r
                    x_scr[b, 2 * vp + 1, pl.ds(row, SUBLANES), :] = n_i
                    xs[k], xs[k + 1] = n_r, n_i
        return tuple(xs)

    init = tuple(st_scr[b, v] for b in range(nb) for v in range(N_STATE_VREGS))
    final = lax.fori_loop(0, nt, steps, init)
    for b in range(nb):
        for v in range(N_STATE_VREGS):
            st_scr[b, v] = final[b * N_STATE_VREGS + v]

    for b in range(nb):
        for s in range(N_SLABS):
            vp, base = s // 2, (s % 2) * STATE_CHUNKS
            parts = [x_scr[b, 2 * vp + ri, pl.ds(base + c, lc, stride=SUBLANES), :]
                     for ri in range(2) for c in range(STATE_CHUNKS)]
            xs = jnp.concatenate(parts, axis=1).astype(BF16)
            y = jnp.dot(xs, wc_ref[s], preferred_element_type=F32)
            sl = slice(s * LANES, (s + 1) * LANES)
            y = y + d_ref[:, sl] * u_ref[b, :, sl].astype(F32)
            y_ref[b, :, sl] = _gelu_tanh(y).astype(BF16)


def _ssm(u, wb, ar, ai, wc, d, *, lc):
    nb, l, _ = u.shape
    full = lambda a: pl.BlockSpec(a.shape, lambda c: (0,) * a.ndim)
    blk = pl.BlockSpec((nb, lc, SSM_WIDTH), lambda c: (0, c, 0))
    return pl.pallas_call(
        functools.partial(_ssm_kernel, nb=nb, lc=lc),
        grid=(l // lc,),
        in_specs=[blk, full(wb), full(ar), full(ai), full(wc), full(d)],
        out_specs=blk,
        out_shape=jax.ShapeDtypeStruct(u.shape, BF16),
        scratch_shapes=[
            pltpu.VMEM((nb, N_STATE_VREGS, lc // SUBLANES, SUBLANES * SUBLANES, LANES), F32),
            pltpu.VMEM((nb, N_STATE_VREGS, lc * SUBLANES, LANES), F32),
            pltpu.VMEM((nb, N_STATE_VREGS, SUBLANES, LANES), F32),
        ],
        compiler_params=pltpu.CompilerParams(
            dimension_semantics=("arbitrary",), vmem_limit_bytes=VMEM_LIMIT),
        name="ssm",
    )(u, wb, ar, ai, wc, d)


def _softplus2(z):
    return jnp.maximum(z, 0.0) + jnp.log2(1.0 + jnp.exp2(-jnp.abs(z)))


def _attn_kernel(q_ref, k_ref, v_ref, tri_ref, o_ref, *, tb):
    qi = pl.program_id(2)
    q = q_ref[0]
    lane = lax.broadcasted_iota(jnp.int32, (tb, LANES), 1)
    r_idx = lax.broadcasted_iota(jnp.int32, (tb, tb), 0)
    c_idx = lax.broadcasted_iota(jnp.int32, (tb, tb), 1)
    causal = c_idx < r_idx
    tri = tri_ref[...]

    def scores(qm, start):
        kb = k_ref[0, pl.ds(start, tb), :]
        return lax.dot_general(qm, kb, (((1,), (1,)), ((), ())), preferred_element_type=F32)

    outs = []
    for hh in range(2):
        own = (lane >= HEAD_DIM) if hh == 1 else (lane < HEAD_DIM)
        qm = jnp.where(own, q, jnp.zeros_like(q))

        start = pl.multiple_of(qi * tb, tb)
        z = scores(qm, start)
        sp = jnp.where(causal, _softplus2(z), 0.0)
        cum = jnp.dot(sp.astype(BF16), tri, preferred_element_type=F32)
        w = jnp.where(causal, jnp.exp2(z - cum), 0.0)
        acc = jnp.dot(w.astype(BF16), v_ref[0, pl.ds(start, tb), :], preferred_element_type=F32)
        carry = cum[:, 0:1]

        def cond(st):
            j, carry, _ = st
            return jnp.logical_and(j >= 0, jnp.min(carry) < EXIT_LOG2)

        def body(st):
            j, carry, acc = st
            start = pl.multiple_of(j * tb, tb)
            z = scores(qm, start)
            sp = _softplus2(z)
            cum = jnp.dot(sp.astype(BF16), tri, preferred_element_type=F32)
            w = jnp.exp2(z - cum - carry)
            acc = acc + jnp.dot(w.astype(BF16), v_ref[0, pl.ds(start, tb), :], preferred_element_type=F32)
            return j - 1, carry + cum[:, 0:1], acc

        _, _, acc = lax.while_loop(cond, body, (qi - 1, carry, acc))
        outs.append(acc)

    o_ref[0] = jnp.where(lane < HEAD_DIM, outs[0], outs[1]).astype(o_ref.dtype)


def _attention(q, k, v, *, tb):
    nb, l, w = q.shape
    tri = (lax.broadcasted_iota(jnp.int32, (tb, tb), 0) >= lax.broadcasted_iota(jnp.int32, (tb, tb), 1)).astype(BF16)
    qspec = pl.BlockSpec((1, tb, LANES), lambda b, hp, i: (b, i, hp))
    kvspec = pl.BlockSpec((1, l, LANES), lambda b, hp, i: (b, 0, hp))
    return pl.pallas_call(
        functools.partial(_attn_kernel, tb=tb),
        grid=(nb, w // LANES, l // tb),
        in_specs=[qspec, kvspec, kvspec, pl.BlockSpec((tb, tb), lambda b, hp, i: (0, 0))],
        out_specs=qspec,
        out_shape=jax.ShapeDtypeStruct(q.shape, BF16),
        compiler_params=pltpu.CompilerParams(
            dimension_semantics=("arbitrary", "arbitrary", "arbitrary"), vmem_limit_bytes=VMEM_LIMIT),
        name="attention",
    )(q, k, v, tri)


def _out_kernel(x_ref, yg_ref, zs_ref, ya_ref, za_ref, gs_ref, ga_ref,
                wglu_ref, bglu_ref, wbs_ref, wba_ref, wout_ref, g_ref, o_ref):
    gl = jnp.dot(yg_ref[...], wglu_ref[...], preferred_element_type=F32) + bglu_ref[...]
    ys = gl[:, :SSM_WIDTH] * _sigmoid(gl[:, SSM_WIDTH:]) * zs_ref[...].astype(F32)
    ya = ya_ref[...].astype(F32) * za_ref[...].astype(F32)
    merged = (gs_ref[...].astype(F32) * jnp.dot(ys.astype(BF16), wbs_ref[...], preferred_element_type=F32)
              + ga_ref[...].astype(F32) * jnp.dot(ya.astype(BF16), wba_ref[...], preferred_element_type=F32))
    out = jnp.dot(merged.astype(BF16), wout_ref[...], preferred_element_type=F32)
    ms = jnp.mean(out * out, axis=-1, keepdims=True)
    o_ref[...] = x_ref[...] + out * lax.rsqrt(ms + EPS) * g_ref[...]


def _out_block(x, yg, zs, ya, za, gs, ga, wglu, bglu, wbs, wba, wout, g, *, tm):
    t = x.shape[0]
    row = lambda n: pl.BlockSpec((tm, n), lambda i: (i, 0))
    full = lambda a: pl.BlockSpec(a.shape, lambda i: (0,) * a.ndim)
    weights = (wglu, bglu, wbs, wba, wout, g)
    return pl.pallas_call(
        _out_kernel,
        grid=(t // tm,),
        in_specs=[row(D_MODEL), row(512), row(512), row(512), row(512), row(D_MODEL), row(D_MODEL)]
                 + [full(a) for a in weights],
        out_specs=row(D_MODEL),
        out_shape=jax.ShapeDtypeStruct(x.shape, F32),
        compiler_params=pltpu.CompilerParams(
            dimension_semantics=("arbitrary",), vmem_limit_bytes=VMEM_LIMIT),
        name="out_block",
    )(x, yg, zs, ya, za, gs, ga, *weights)


def _tile(n, target):
    t = min(n, target)
    assert n % t == 0, (n, target)
    return t


def kernel(x, pre_norm_g, post_norm_g, w_in, ssm_a_re, ssm_a_im, ssm_log_dt, ssm_b_re, ssm_b_im,
           ssm_c_re, ssm_c_im, ssm_d, w_glu, b_glu, w_branch_ssm, w_branch_attn, w_out):
    nb, l, d = x.shape
    depth = w_in.shape[0]
    assert d == D_MODEL and w_in.shape[2] == IN_COLS
    t = nb * l
    tm = _tile(t, 512)
    lc = _tile(l, 256)
    tb = _tile(l, 256)

    abar_re, abar_im, bb_re, bb_im = _ssm_discretise(
        ssm_a_re, ssm_a_im, ssm_log_dt[..., None],
        ssm_b_re.transpose(0, 3, 1, 2), ssm_b_im.transpose(0, 3, 1, 2))
    bb_re = bb_re.transpose(0, 2, 1, 3)
    bb_im = bb_im.transpose(0, 2, 1, 3)
    xt = x.reshape(t, d)
    for li in range(depth):
        wb = jnp.concatenate([_block_diag_slabs(bb_re[li]), _block_diag_slabs(bb_im[li])], axis=2).astype(BF16)
        wc = jnp.concatenate([_block_diag_slabs(ssm_c_re[li].transpose(0, 2, 1)),
                              _block_diag_slabs(-ssm_c_im[li].transpose(0, 2, 1))], axis=1).astype(BF16)
        ar = abar_re[li].reshape(2, SUBLANES, LANES)
        ai = abar_im[li].reshape(2, SUBLANES, LANES)

        u, zs, q, k, v, za, gs, ga = _in_proj(xt, pre_norm_g[li][None, :], w_in[li].astype(BF16), tm=tm)
        yg = _ssm(u.reshape(nb, l, SSM_WIDTH), wb, ar, ai, wc, ssm_d[li][None, :], lc=lc)
        ya = _attention(q.reshape(nb, l, ATTN_WIDTH), k.reshape(nb, l, ATTN_WIDTH),
                        v.reshape(nb, l, ATTN_WIDTH), tb=tb)
        xt = _out_block(xt, yg.reshape(t, SSM_WIDTH), zs, ya.reshape(t, ATTN_WIDTH), za, gs, ga,
                        w_glu[li].astype(BF16), b_glu[li][None, :], w_branch_ssm[li].astype(BF16),
                        w_branch_attn[li].astype(BF16), w_out[li].astype(BF16), post_norm_g[li][None, :], tm=tm)
    return xt.reshape(nb, l, d)
```

```python
import functools
import math

import jax
import jax.numpy as jnp
from jax import lax
from jax.experimental import pallas as pl
from jax.experimental.pallas import tpu as pltpu

F32 = jnp.float32
BF16 = jnp.bfloat16

D_MODEL = 1024
SSM_WIDTH = 512
SSM_GROUP = 16
SSM_GROUPS = 32
SSM_STATE = 64
ATTN_WIDTH = 512
HEAD_DIM = 64
EPS = 1e-6
IN_COLS = 5120

LANES = 128
SUBLANES = 8
N_SLABS = SSM_WIDTH // LANES
SLAB_STATE = (LANES // SSM_GROUP) * SSM_STATE
STATE_CHUNKS = SLAB_STATE // LANES
N_STATE_VREGS = 4
N_HEAD_PAIRS = ATTN_WIDTH // LANES

VMEM_LIMIT = 56 * 1024 * 1024

LOG2E = math.log2(math.e)
Q_SCALE = HEAD_DIM ** -0.5 * LOG2E
EXIT_LOG2 = 150.0
NO_BLOCK = 1e30


def _sigmoid(x):
    return 1.0 / (1.0 + jnp.exp(-x))


def _silu(x):
    return x * _sigmoid(x)


def _gelu_tanh(x):
    return 0.5 * x * (1.0 + jnp.tanh(math.sqrt(2.0 / math.pi) * (x + 0.044715 * (x * x * x))))


def _in_proj_kernel(x_ref, g_ref, w_ref, u_ref, zs_ref, q_ref, k_ref, v_ref, za_ref, gs_ref, ga_ref):
    x = x_ref[...]
    ms = jnp.mean(x * x, axis=-1, keepdims=True)
    h = (x * lax.rsqrt(ms + EPS) * g_ref[...]).astype(BF16)

    def proj(lo, hi):
        return jnp.dot(h, w_ref[:, lo:hi], preferred_element_type=F32)

    u_ref[...] = proj(0, 512).astype(BF16)
    zs_ref[...] = _silu(proj(512, 1024)).astype(BF16)
    for ref, lo, scale in ((q_ref, 1024, Q_SCALE), (k_ref, 1536, None), (v_ref, 2048, None)):
        y = proj(lo, lo + ATTN_WIDTH)
        if scale is not None:
            y = y * scale
        for hp in range(N_HEAD_PAIRS):
            ref[hp] = y[:, hp * LANES:(hp + 1) * LANES].astype(BF16)
    za_ref[...] = _silu(proj(2560, 3072)).astype(BF16)
    gs_ref[...] = _sigmoid(proj(3072, 4096)).astype(BF16)
    ga_ref[...] = _sigmoid(proj(4096, 5120)).astype(BF16)


def _layer_spec(a, li):
    return pl.BlockSpec((None,) + a.shape[1:], lambda *_: (li,) + (0,) * (a.ndim - 1))


def _in_proj(x, g, w, li, *, tm):
    t = x.shape[0]
    row = lambda n: pl.BlockSpec((tm, n), lambda i: (i, 0))
    full = lambda a: _layer_spec(a, li)
    heads = pl.BlockSpec((N_HEAD_PAIRS, tm, LANES), lambda i: (0, i, 0))
    heads_shape = jax.ShapeDtypeStruct((N_HEAD_PAIRS, t, LANES), BF16)
    flat = lambda n: jax.ShapeDtypeStruct((t, n), BF16)
    return pl.pallas_call(
        _in_proj_kernel,
        grid=(t // tm,),
        in_specs=[row(D_MODEL), full(g), full(w)],
        out_specs=[row(512), row(512), heads, heads, heads, row(512), row(1024), row(1024)],
        out_shape=[flat(512), flat(512), heads_shape, heads_shape, heads_shape, flat(512), flat(1024), flat(1024)],
        compiler_params=pltpu.CompilerParams(
            dimension_semantics=("arbitrary",), vmem_limit_bytes=VMEM_LIMIT),
        name="in_proj",
    )(x, g, w)


def _ssm_disc_kernel(are_ref, aim_ref, ldt_ref, bre_ref, bim_ref, abr_ref, abi_ref, bbr_ref, bbi_ref):
    a_re = are_ref[0]
    a_im = aim_ref[0]
    dt = jnp.exp(ldt_ref[0])
    mag = jnp.exp(a_re * dt)
    abar_re = mag * jnp.cos(a_im * dt)
    abar_im = mag * jnp.sin(a_im * dt)
    nr = abar_re - 1.0
    ni = abar_im
    den = a_re * a_re + a_im * a_im
    f_re = (nr * a_re + ni * a_im) / den
    f_im = (ni * a_re - nr * a_im) / den
    abr_ref[0] = abar_re
    abi_ref[0] = abar_im
    for h in range(SSM_GROUP):
        b_re = bre_ref[0, h]
        b_im = bim_ref[0, h]
        bbr_ref[0, h] = f_re * b_re - f_im * b_im
        bbi_ref[0, h] = f_re * b_im + f_im * b_re


def _ssm_discretise(a_re, a_im, log_dt, b_re_t, b_im_t):
    depth = a_re.shape[0]
    spec = lambda a: pl.BlockSpec((1,) + a.shape[1:], lambda l: (l,) + (0,) * (a.ndim - 1))
    ins = (a_re, a_im, log_dt, b_re_t, b_im_t)
    outs = (a_re, a_im, b_re_t, b_im_t)
    return pl.pallas_call(
        _ssm_disc_kernel,
        grid=(depth,),
        in_specs=[spec(a) for a in ins],
        out_specs=[spec(a) for a in outs],
        out_shape=[jax.ShapeDtypeStruct(a.shape, F32) for a in outs],
        name="ssm_discretise",
    )(*ins)


def _block_diag_slabs(w):
    depth, g, a, b = w.shape
    gl = g // N_SLABS
    w = w.reshape(depth, N_SLABS, gl, a, b)
    eye = jnp.eye(gl, dtype=w.dtype)
    out = w[:, :, :, :, None, :] * eye[None, None, :, None, :, None]
    return out.reshape(depth, N_SLABS, gl * a, gl * b)


def _ssm_kernel(u_ref, wb_ref, ar_ref, ai_ref, wc_ref, d_ref, y_ref, bu_scr, x_scr, st_scr, *, nb, lc):
    @pl.when(pl.program_id(0) == 0)
    def _():
        st_scr[...] = jnp.zeros_like(st_scr)

    nt = lc // SUBLANES

    for b in range(nb):
        for s in range(N_SLABS):
            bu = jnp.dot(u_ref[b, :, s * LANES:(s + 1) * LANES], wb_ref[s], preferred_element_type=F32)
            vp, base = s // 2, (s % 2) * STATE_CHUNKS
            for c in range(STATE_CHUNKS):
                j = base + c
                re = bu[:, c * LANES:(c + 1) * LANES]
                im = bu[:, SLAB_STATE + c * LANES:SLAB_STATE + (c + 1) * LANES]
                bu_scr[b, 2 * vp, :, j * SUBLANES:(j + 1) * SUBLANES, :] = re.reshape(nt, SUBLANES, LANES)
                bu_scr[b, 2 * vp + 1, :, j * SUBLANES:(j + 1) * SUBLANES, :] = im.reshape(nt, SUBLANES, LANES)

    a_r = [ar_ref[0], ar_ref[1]]
    a_i = [ai_ref[0], ai_ref[1]]

    def steps(i, carry):
        xs = list(carry)
        for r in range(SUBLANES):
            row = pl.multiple_of(i * (SUBLANES * SUBLANES) + r * SUBLANES, SUBLANES)
            for b in range(nb):
                for vp in range(2):
                    k = (b * 2 + vp) * 2
                    b_r = bu_scr[b, 2 * vp, i, pl.ds(r, SUBLANES, stride=SUBLANES), :]
                    b_i = bu_scr[b, 2 * vp + 1, i, pl.ds(r, SUBLANES, stride=SUBLANES), :]
                    x_r, x_i = xs[k], xs[k + 1]
                    n_r = a_r[vp] * x_r - a_i[vp] * x_i + b_r
                    n_i = a_r[vp] * x_i + a_i[vp] * x_r + b_i
                    x_scr[b, 2 * vp, pl.ds(row, SUBLANES), :] = n_r
                    x_scr[b, 2 * vp + 1, pl.ds(row, SUBLANES), :] = n_i
                    xs[k], xs[k + 1] = n_r, n_i
        return tuple(xs)

    init = tuple(st_scr[b, v] for b in range(nb) for v in range(N_STATE_VREGS))
    final = lax.fori_loop(0, nt, steps, init)
    for b in range(nb):
        for v in range(N_STATE_VREGS):
            st_scr[b, v] = final[b * N_STATE_VREGS + v]

    for b in range(nb):
        for s in range(N_SLABS):
            vp, base = s // 2, (s % 2) * STATE_CHUNKS
            parts = [x_scr[b, 2 * vp + ri, pl.ds(base + c, lc, stride=SUBLANES), :]
                     for ri in range(2) for c in range(STATE_CHUNKS)]
            xs = jnp.concatenate(parts, axis=1).astype(BF16)
            y = jnp.dot(xs, wc_ref[s], preferred_element_type=F32)
            sl = slice(s * LANES, (s + 1) * LANES)
            y = y + d_ref[:, sl] * u_ref[b, :, sl].astype(F32)
            y_ref[b, :, sl] = _gelu_tanh(y).astype(BF16)


def _ssm(u, wb, ar, ai, wc, d, li, *, lc):
    nb, l, _ = u.shape
    full = lambda a: _layer_spec(a, li)
    blk = pl.BlockSpec((nb, lc, SSM_WIDTH), lambda c: (0, c, 0))
    return pl.pallas_call(
        functools.partial(_ssm_kernel, nb=nb, lc=lc),
        grid=(l // lc,),
        in_specs=[blk, full(wb), full(ar), full(ai), full(wc), full(d)],
        out_specs=blk,
        out_shape=jax.ShapeDtypeStruct(u.shape, BF16),
        scratch_shapes=[
            pltpu.VMEM((nb, N_STATE_VREGS, lc // SUBLANES, SUBLANES * SUBLANES, LANES), F32),
            pltpu.VMEM((nb, N_STATE_VREGS, lc * SUBLANES, LANES), F32),
            pltpu.VMEM((nb, N_STATE_VREGS, SUBLANES, LANES), F32),
        ],
        compiler_params=pltpu.CompilerParams(
            dimension_semantics=("arbitrary",), vmem_limit_bytes=VMEM_LIMIT),
        name="ssm",
    )(u, wb, ar, ai, wc, d)


def _softplus2(z):
    return jnp.maximum(z, 0.0) + jnp.log2(1.0 + jnp.exp2(-jnp.abs(z)))


def _attn_kernel(q_ref, k_ref, v_ref, tri_ref, o_ref, acc_scr, carry_scr, *, tb):
    qi = pl.program_id(1)
    low = lax.broadcasted_iota(jnp.int32, (tb, LANES), 1) < HEAD_DIM
    causal = (lax.broadcasted_iota(jnp.int32, (tb, tb), 1)
              < lax.broadcasted_iota(jnp.int32, (tb, tb), 0))
    tri = tri_ref[...]
    start0 = pl.multiple_of(qi * tb, tb)
    start1 = pl.multiple_of(jnp.maximum(qi - 1, 0) * tb, tb)
    no_prev = jnp.where(qi > 0, 0.0, NO_BLOCK)

    causal2 = jnp.concatenate([causal, causal], axis=0)

    def stacked_q(hp):
        q = q_ref[hp]
        zero = jnp.zeros_like(q)
        return jnp.concatenate([jnp.where(low, q, zero), jnp.where(low, zero, q)], axis=0)

    def scores(hp, q2, start):
        return lax.dot_general(q2, k_ref[hp, pl.ds(start, tb), :], (((1,), (1,)), ((), ())),
                               preferred_element_type=F32)

    def weighted_v(hp, w, start):
        return jnp.dot(w.astype(BF16), v_ref[hp, pl.ds(start, tb), :], preferred_element_type=F32)

    def unstack(x):
        return jnp.where(low, x[:tb], x[tb:])

    def near(hp, min_carry):
        q2 = stacked_q(hp)
        z0 = scores(hp, q2, start0)
        z1 = scores(hp, q2, start1)
        sp = jnp.concatenate([jnp.where(causal2, _softplus2(z0), 0.0), _softplus2(z1)], axis=0)
        cum = jnp.dot(sp.astype(BF16), tri, preferred_element_type=F32)
        cum0, cum1 = cum[:2 * tb], cum[2 * tb:]
        c0 = cum0[:, 0:1]
        w0 = jnp.where(causal2, jnp.exp2(z0 - cum0), 0.0)
        w1 = jnp.exp2(z1 - cum1 - (c0 + no_prev))
        out = unstack(weighted_v(hp, w0, start0) + weighted_v(hp, w1, start1))
        carry = c0 + cum1[:, 0:1]
        carry_scr[hp] = carry
        acc_scr[hp] = out
        o_ref[hp] = out.astype(o_ref.dtype)
        return jnp.minimum(min_carry, jnp.minimum(carry[:tb], carry[tb:]))

    min_carry = jnp.full((tb, 1), jnp.inf, F32)
    for hp in range(N_HEAD_PAIRS):
        min_carry = near(hp, min_carry)

    @pl.when(jnp.logical_and(qi >= 2, jnp.min(min_carry) < EXIT_LOG2))
    def _():
        def far(hp, _):
            q2 = stacked_q(hp)

            def cond(st):
                j, carry, _ = st
                return jnp.logical_and(j >= 0, jnp.min(carry) < EXIT_LOG2)

            def body(st):
                j, carry, acc = st
                start = pl.multiple_of(j * tb, tb)
                z = scores(hp, q2, start)
                cum = jnp.dot(_softplus2(z).astype(BF16), tri, preferred_element_type=F32)
                w = jnp.exp2(z - cum - carry)
                return j - 1, carry + cum[:, 0:1], acc + weighted_v(hp, w, start)

            init = (qi - 2, carry_scr[hp], jnp.zeros((2 * tb, LANES), F32))
            acc = lax.while_loop(cond, body, init)[2]
            o_ref[hp] = (acc_scr[hp] + unstack(acc)).astype(o_ref.dtype)
            return 0

        lax.fori_loop(0, N_HEAD_PAIRS, far, 0)


def _attention(q, k, v, *, nb, tb):
    nhp, t, _ = q.shape
    l = t // nb
    nq = l // tb
    tri = (lax.broadcasted_iota(jnp.int32, (tb, tb), 0) >= lax.broadcasted_iota(jnp.int32, (tb, tb), 1)).astype(BF16)
    qspec = pl.BlockSpec((nhp, tb, LANES), lambda b, i: (0, b * nq + i, 0))
    kvspec = pl.BlockSpec((nhp, l, LANES), lambda b, i: (0, b, 0))
    return pl.pallas_call(
        functools.partial(_attn_kernel, tb=tb),
        grid=(nb, nq),
        in_specs=[qspec, kvspec, kvspec, pl.BlockSpec((tb, tb), lambda b, i: (0, 0))],
        out_specs=qspec,
        out_shape=jax.ShapeDtypeStruct(q.shape, BF16),
        scratch_shapes=[pltpu.VMEM((nhp, tb, LANES), F32), pltpu.VMEM((nhp, 2 * tb, 1), F32)],
        compiler_params=pltpu.CompilerParams(
            dimension_semantics=("arbitrary", "arbitrary"), vmem_limit_bytes=VMEM_LIMIT),
        name="attention",
    )(q, k, v, tri)


def _out_kernel(x_ref, yg_ref, zs_ref, ya_ref, za_ref, gs_ref, ga_ref,
                wglu_ref, bglu_ref, wbs_ref, wba_ref, wout_ref, g_ref, o_ref):
    gl = jnp.dot(yg_ref[...], wglu_ref[...], preferred_element_type=F32) + bglu_ref[...]
    ys = gl[:, :SSM_WIDTH] * _sigmoid(gl[:, SSM_WIDTH:]) * zs_ref[...].astype(F32)
    ya = jnp.concatenate([ya_ref[hp] for hp in range(N_HEAD_PAIRS)], axis=1).astype(F32) * za_ref[...].astype(F32)
    merged = (gs_ref[...].astype(F32) * jnp.dot(ys.astype(BF16), wbs_ref[...], preferred_element_type=F32)
              + ga_ref[...].astype(F32) * jnp.dot(ya.astype(BF16), wba_ref[...], preferred_element_type=F32))
    out = jnp.dot(merged.astype(BF16), wout_ref[...], preferred_element_type=F32)
    ms = jnp.mean(out * out, axis=-1, keepdims=True)
    o_ref[...] = x_ref[...] + out * lax.rsqrt(ms + EPS) * g_ref[...]


def _out_block(x, yg, zs, ya, za, gs, ga, wglu, bglu, wbs, wba, wout, g, li, *, tm):
    t = x.shape[0]
    row = lambda n: pl.BlockSpec((tm, n), lambda i: (i, 0))
    full = lambda a: _layer_spec(a, li)
    weights = (wglu, bglu, wbs, wba, wout, g)
    return pl.pallas_call(
        _out_kernel,
        grid=(t // tm,),
        in_specs=[row(D_MODEL), row(512), row(512),
                  pl.BlockSpec((N_HEAD_PAIRS, tm, LANES), lambda i: (0, i, 0)),
                  row(512), row(D_MODEL), row(D_MODEL)] + [full(a) for a in weights],
        out_specs=row(D_MODEL),
        out_shape=jax.ShapeDtypeStruct(x.shape, F32),
        compiler_params=pltpu.CompilerParams(
            dimension_semantics=("arbitrary",), vmem_limit_bytes=VMEM_LIMIT),
        name="out_block",
    )(x, yg, zs, ya, za, gs, ga, *weights)


def _tile(n, target):
    t = min(n, target)
    assert n % t == 0, (n, target)
    return t


def kernel(x, pre_norm_g, post_norm_g, w_in, ssm_a_re, ssm_a_im, ssm_log_dt, ssm_b_re, ssm_b_im,
           ssm_c_re, ssm_c_im, ssm_d, w_glu, b_glu, w_branch_ssm, w_branch_attn, w_out):
    nb, l, d = x.shape
    depth = w_in.shape[0]
    assert d == D_MODEL and w_in.shape[2] == IN_COLS
    t = nb * l
    tm = _tile(t, 512)
    lc = _tile(l, 256)
    tb = _tile(l, 256)

    abar_re, abar_im, bb_re, bb_im = _ssm_discretise(
        ssm_a_re, ssm_a_im, ssm_log_dt[..., None],
        ssm_b_re.transpose(0, 3, 1, 2), ssm_b_im.transpose(0, 3, 1, 2))
    bb_re = bb_re.transpose(0, 2, 1, 3)
    bb_im = bb_im.transpose(0, 2, 1, 3)
    wb = jnp.concatenate([_block_diag_slabs(bb_re), _block_diag_slabs(bb_im)], axis=3).astype(BF16)
    wc = jnp.concatenate([_block_diag_slabs(ssm_c_re.transpose(0, 1, 3, 2)),
                          _block_diag_slabs(-ssm_c_im.transpose(0, 1, 3, 2))], axis=2).astype(BF16)
    ar = abar_re.reshape(depth, 2, SUBLANES, LANES)
    ai = abar_im.reshape(depth, 2, SUBLANES, LANES)

    row = lambda a: a[:, None, :]
    w_in_b, w_glu_b, w_bs_b, w_ba_b, w_out_b = (
        a.astype(BF16) for a in (w_in, w_glu, w_branch_ssm, w_branch_attn, w_out))
    pre_g, post_g, d_skip, b_glu_r = row(pre_norm_g), row(post_norm_g), row(ssm_d), row(b_glu)

    xt = x.reshape(t, d)
    for li in range(depth):
        u, zs, q, k, v, za, gs, ga = _in_proj(xt, pre_g, w_in_b, li, tm=tm)
        yg = _ssm(u.reshape(nb, l, SSM_WIDTH), wb, ar, ai, wc, d_skip, li, lc=lc)
        ya = _attention(q, k, v, nb=nb, tb=tb)
        xt = _out_block(xt, yg.reshape(t, SSM_WIDTH), zs, ya, za, gs, ga,
                        w_glu_b, b_glu_r, w_bs_b, w_ba_b, w_out_b, post_g, li, tm=tm)
    return xt.reshape(nb, l, d)
```

```python
import functools
import math

import jax
import jax.numpy as jnp
from jax import lax
from jax.experimental import pallas as pl
from jax.experimental.pallas import tpu as pltpu

F32 = jnp.float32
BF16 = jnp.bfloat16

D_MODEL = 1024
SSM_WIDTH = 512
SSM_GROUP = 16
SSM_GROUPS = 32
SSM_STATE = 64
ATTN_WIDTH = 512
HEAD_DIM = 64
EPS = 1e-6
IN_COLS = 5120

LANES = 128
SUBLANES = 8
N_SLABS = SSM_WIDTH // LANES
SLAB_STATE = (LANES // SSM_GROUP) * SSM_STATE
STATE_CHUNKS = SLAB_STATE // LANES
N_STATE_VREGS = 4
N_HEAD_PAIRS = ATTN_WIDTH // LANES

VMEM_LIMIT = 56 * 1024 * 1024

LOG2E = math.log2(math.e)
Q_SCALE = HEAD_DIM ** -0.5 * LOG2E
EXIT_LOG2 = 150.0
NO_BLOCK = 1e30


def _sigmoid(x):
    return 1.0 / (1.0 + jnp.exp(-x))


def _silu(x):
    return x * _sigmoid(x)


def _gelu_tanh(x):
    return 0.5 * x * (1.0 + jnp.tanh(math.sqrt(2.0 / math.pi) * (x + 0.044715 * (x * x * x))))


def _in_proj_kernel(x_ref, g_ref, w_ref, u_ref, zs_ref, q_ref, k_ref, v_ref, za_ref, gs_ref, ga_ref):
    x = x_ref[...]
    ms = jnp.mean(x * x, axis=-1, keepdims=True)
    h = (x * lax.rsqrt(ms + EPS) * g_ref[...]).astype(BF16)

    def proj(lo, hi):
        return jnp.dot(h, w_ref[:, lo:hi].astype(BF16), preferred_element_type=F32)

    u_ref[...] = proj(0, 512).astype(BF16)
    zs_ref[...] = _silu(proj(512, 1024)).astype(BF16)
    for ref, lo, scale in ((q_ref, 1024, Q_SCALE), (k_ref, 1536, None), (v_ref, 2048, None)):
        y = proj(lo, lo + ATTN_WIDTH)
        if scale is not None:
            y = y * scale
        for hp in range(N_HEAD_PAIRS):
            ref[hp] = y[:, hp * LANES:(hp + 1) * LANES].astype(BF16)
    za_ref[...] = _silu(proj(2560, 3072)).astype(BF16)
    gs_ref[...] = _sigmoid(proj(3072, 4096)).astype(BF16)
    ga_ref[...] = _sigmoid(proj(4096, 5120)).astype(BF16)


def _layer_spec(a, li):
    return pl.BlockSpec((None,) + a.shape[1:], lambda *_: (li,) + (0,) * (a.ndim - 1),
                        pipeline_mode=pl.Buffered(1))


def _in_proj(x, g, w, li, *, nb, tm):
    t = x.shape[0]
    npb = t // nb // tm
    row = lambda n: pl.BlockSpec((tm, n), lambda i: (i, 0))
    seq = pl.BlockSpec((None, tm, SSM_WIDTH), lambda i: (i // npb, i % npb, 0))
    full = lambda a: _layer_spec(a, li)
    heads = pl.BlockSpec((N_HEAD_PAIRS, tm, LANES), lambda i: (0, i, 0))
    heads_shape = jax.ShapeDtypeStruct((N_HEAD_PAIRS, t, LANES), BF16)
    flat = lambda n: jax.ShapeDtypeStruct((t, n), BF16)
    return pl.pallas_call(
        _in_proj_kernel,
        grid=(t // tm,),
        in_specs=[row(D_MODEL), full(g), full(w)],
        out_specs=[seq, row(512), heads, heads, heads, row(512), row(1024), row(1024)],
        out_shape=[jax.ShapeDtypeStruct((nb, t // nb, SSM_WIDTH), BF16), flat(512),
                   heads_shape, heads_shape, heads_shape, flat(512), flat(1024), flat(1024)],
        compiler_params=pltpu.CompilerParams(
            dimension_semantics=("arbitrary",), vmem_limit_bytes=VMEM_LIMIT),
        name="in_proj",
    )(x, g, w)


def _ssm_disc_kernel(are_ref, aim_ref, ldt_ref, bre_ref, bim_ref, abr_ref, abi_ref, bbr_ref, bbi_ref):
    a_re = are_ref[0]
    a_im = aim_ref[0]
    dt = jnp.exp(ldt_ref[0])
    mag = jnp.exp(a_re * dt)
    abar_re = mag * jnp.cos(a_im * dt)
    abar_im = mag * jnp.sin(a_im * dt)
    nr = abar_re - 1.0
    ni = abar_im
    den = a_re * a_re + a_im * a_im
    f_re = (nr * a_re + ni * a_im) / den
    f_im = (ni * a_re - nr * a_im) / den
    abr_ref[0] = abar_re
    abi_ref[0] = abar_im
    for h in range(SSM_GROUP):
        b_re = bre_ref[0, h]
        b_im = bim_ref[0, h]
        bbr_ref[0, h] = f_re * b_re - f_im * b_im
        bbi_ref[0, h] = f_re * b_im + f_im * b_re


def _ssm_discretise(a_re, a_im, log_dt, b_re_t, b_im_t):
    depth = a_re.shape[0]
    spec = lambda a: pl.BlockSpec((1,) + a.shape[1:], lambda l: (l,) + (0,) * (a.ndim - 1))
    ins = (a_re, a_im, log_dt, b_re_t, b_im_t)
    outs = (a_re, a_im, b_re_t, b_im_t)
    return pl.pallas_call(
        _ssm_disc_kernel,
        grid=(depth,),
        in_specs=[spec(a) for a in ins],
        out_specs=[spec(a) for a in outs],
        out_shape=[jax.ShapeDtypeStruct(a.shape, F32) for a in outs],
        name="ssm_discretise",
    )(*ins)


def _block_diag_slabs(w):
    depth, g, a, b = w.shape
    gl = g // N_SLABS
    w = w.reshape(depth, N_SLABS, gl, a, b)
    eye = jnp.eye(gl, dtype=w.dtype)
    out = w[:, :, :, :, None, :] * eye[None, None, :, None, :, None]
    return out.reshape(depth, N_SLABS, gl * a, gl * b)


def _ssm_kernel(u_in_ref, u_out_ref, wb_ref, ar_ref, ai_ref, wc_ref, d_ref, y_ref,
                bu0, bu1, x0, x1, st_scr, *, nb, lc):
    g = pl.program_id(0)
    nt = lc // SUBLANES

    @pl.when(g == 0)
    def _():
        st_scr[...] = jnp.zeros_like(st_scr)
        bu1[...] = jnp.zeros_like(bu1)
        x0[...] = jnp.zeros_like(x0)

    def input_piece(bu_scr, b, s, half):
        ri, c0 = divmod(half * 2, STATE_CHUNKS)
        cols = slice(half * 2 * LANES, (half + 1) * 2 * LANES)
        bu = jnp.dot(u_in_ref[b, :, s * LANES:(s + 1) * LANES], wb_ref[s, :, cols], preferred_element_type=F32)
        vp, base = s // 2, (s % 2) * STATE_CHUNKS
        for c in range(2):
            j = base + c0 + c
            bu_scr[b, 2 * vp + ri, :, j * SUBLANES:(j + 1) * SUBLANES, :] = (
                bu[:, c * LANES:(c + 1) * LANES].reshape(nt, SUBLANES, LANES))

    def timesteps(bu_scr, x_scr, xs, t0, t1):
        for t in range(t0, t1):
            i, r = divmod(t, SUBLANES)
            for b in range(nb):
                for vp in range(2):
                    k = (b * 2 + vp) * 2
                    b_r = bu_scr[b, 2 * vp, i, pl.ds(r, SUBLANES, stride=SUBLANES), :]
                    b_i = bu_scr[b, 2 * vp + 1, i, pl.ds(r, SUBLANES, stride=SUBLANES), :]
                    x_r, x_i = xs[k], xs[k + 1]
                    n_r = a_r[vp] * x_r - a_i[vp] * x_i + b_r
                    n_i = a_r[vp] * x_i + a_i[vp] * x_r + b_i
                    x_scr[b, 2 * vp, t * SUBLANES:(t + 1) * SUBLANES, :] = n_r
                    x_scr[b, 2 * vp + 1, t * SUBLANES:(t + 1) * SUBLANES, :] = n_i
                    xs[k], xs[k + 1] = n_r, n_i

    def output_piece(x_scr, acc, b, s, half):
        ri, c0 = divmod(half * 2, STATE_CHUNKS)
        vp, base = s // 2, (s % 2) * STATE_CHUNKS
        parts = [x_scr[b, 2 * vp + ri, pl.ds(base + c0 + c, lc, stride=SUBLANES), :] for c in range(2)]
        xs = jnp.concatenate(parts, axis=1).astype(BF16)
        rows = slice(half * 2 * LANES, (half + 1) * 2 * LANES)
        y = jnp.dot(xs, wc_ref[s, rows, :], preferred_element_type=F32)
        acc = y if acc is None else acc + y
        if half < 3:
            return acc
        sl = slice(s * LANES, (s + 1) * LANES)
        acc = acc + d_ref[:, sl] * u_out_ref[b, :, sl].astype(F32)
        y_ref[b, :, sl] = _gelu_tanh(acc).astype(BF16)
        return None

    a_r = [ar_ref[0], ar_ref[1]]
    a_i = [ai_ref[0], ai_ref[1]]
    pieces = [(b, s, half) for b in range(nb) for s in range(N_SLABS) for half in range(4)]

    def step(bu_w, bu_r, x_w, x_r):
        xs = [st_scr[b, v] for b in range(nb) for v in range(N_STATE_VREGS)]
        acc = None
        n_groups = nt
        for grp in range(n_groups):
            lo, hi = grp * len(pieces) // n_groups, (grp + 1) * len(pieces) // n_groups
            for piece in pieces[lo:hi]:
                input_piece(bu_w, *piece)
            timesteps(bu_r, x_w, xs, grp * lc // n_groups, (grp + 1) * lc // n_groups)
            for piece in pieces[lo:hi]:
                acc = output_piece(x_r, acc, *piece)
        for b in range(nb):
            for v in range(N_STATE_VREGS):
                st_scr[b, v] = xs[b * N_STATE_VREGS + v]

    @pl.when(g % 2 == 0)
    def _():
        step(bu0, bu1, x1, x0)

    @pl.when(g % 2 == 1)
    def _():
        step(bu1, bu0, x0, x1)


def _ssm(u, wb, ar, ai, wc, d, li, *, lc):
    nb, l, _ = u.shape
    nc = l // lc
    full = lambda a: _layer_spec(a, li)
    blk = lambda f: pl.BlockSpec((nb, lc, SSM_WIDTH), lambda g: (0, f(g), 0))
    in_chunk = lambda g: jnp.minimum(g, nc - 1)
    out_chunk = lambda g: jnp.maximum(g - 2, 0)
    bu_shape = pltpu.VMEM((nb, N_STATE_VREGS, lc // SUBLANES, SUBLANES * SUBLANES, LANES), F32)
    x_shape = pltpu.VMEM((nb, N_STATE_VREGS, lc * SUBLANES, LANES), F32)
    return pl.pallas_call(
        functools.partial(_ssm_kernel, nb=nb, lc=lc),
        grid=(nc + 2,),
        in_specs=[blk(in_chunk), blk(out_chunk), full(wb), full(ar), full(ai), full(wc), full(d)],
        out_specs=blk(out_chunk),
        out_shape=jax.ShapeDtypeStruct(u.shape, BF16),
        scratch_shapes=[bu_shape, bu_shape, x_shape, x_shape,
                        pltpu.VMEM((nb, N_STATE_VREGS, SUBLANES, LANES), F32)],
        compiler_params=pltpu.CompilerParams(
            dimension_semantics=("arbitrary",), vmem_limit_bytes=VMEM_LIMIT),
        name="ssm",
    )(u, u, wb, ar, ai, wc, d)


def _softplus2(z):
    return jnp.maximum(z, 0.0) + jnp.log2(1.0 + jnp.exp2(-jnp.abs(z)))


def _attn_kernel(q_ref, k_ref, v_ref, tri_ref, o_ref, acc_scr, carry_scr, *, tb):
    qi = pl.program_id(1)
    low = lax.broadcasted_iota(jnp.int32, (tb, LANES), 1) < HEAD_DIM
    causal = (lax.broadcasted_iota(jnp.int32, (tb, tb), 1)
              < lax.broadcasted_iota(jnp.int32, (tb, tb), 0))
    tri = tri_ref[...]
    start0 = pl.multiple_of(qi * tb, tb)
    start1 = pl.multiple_of(jnp.maximum(qi - 1, 0) * tb, tb)
    no_prev = jnp.where(qi > 0, 0.0, NO_BLOCK)

    causal2 = jnp.concatenate([causal, causal], axis=0)

    def stacked_q(hp):
        q = q_ref[hp]
        zero = jnp.zeros_like(q)
        return jnp.concatenate([jnp.where(low, q, zero), jnp.where(low, zero, q)], axis=0)

    def scores(hp, q2, start):
        return lax.dot_general(q2, k_ref[hp, pl.ds(start, tb), :], (((1,), (1,)), ((), ())),
                               preferred_element_type=F32)

    def weighted_v(hp, w, start):
        return jnp.dot(w.astype(BF16), v_ref[hp, pl.ds(start, tb), :], preferred_element_type=F32)

    def unstack(x):
        return jnp.where(low, x[:tb], x[tb:])

    def stage_a(hp):
        q2 = stacked_q(hp)
        z0 = scores(hp, q2, start0)
        z1 = scores(hp, q2, start1)
        sp = jnp.concatenate([jnp.where(causal2, _softplus2(z0), 0.0), _softplus2(z1)], axis=0)
        return z0, z1, sp.astype(BF16)

    def stage_b(hp, z0, z1, sp):
        cum = jnp.dot(sp, tri, preferred_element_type=F32)
        cum0, cum1 = cum[:2 * tb], cum[2 * tb:]
        c0 = cum0[:, 0:1]
        w0 = jnp.where(causal2, jnp.exp2(z0 - cum0), 0.0).astype(BF16)
        w1 = jnp.exp2(z1 - cum1 - (c0 + no_prev)).astype(BF16)
        return w0, w1, c0 + cum1[:, 0:1]

    def stage_c(hp, w0, w1, carry, min_carry):
        out = unstack(jnp.dot(w0, v_ref[hp, pl.ds(start0, tb), :], preferred_element_type=F32)
                      + jnp.dot(w1, v_ref[hp, pl.ds(start1, tb), :], preferred_element_type=F32))
        carry_scr[hp] = carry
        acc_scr[hp] = out
        o_ref[hp] = out.astype(o_ref.dtype)
        return jnp.minimum(min_carry, jnp.minimum(carry[:tb], carry[tb:]))

    min_carry = jnp.full((tb, 1), jnp.inf, F32)
    a_out, b_out = {}, {}
    for step in range(N_HEAD_PAIRS + 2):
        if step < N_HEAD_PAIRS:
            a_out[step] = stage_a(step)
        if 0 <= step - 1 < N_HEAD_PAIRS:
            b_out[step - 1] = stage_b(step - 1, *a_out.pop(step - 1))
        if 0 <= step - 2 < N_HEAD_PAIRS:
            min_carry = stage_c(step - 2, *b_out.pop(step - 2), min_carry)

    @pl.when(jnp.logical_and(qi >= 2, jnp.min(min_carry) < EXIT_LOG2))
    def _():
        def far(hp, _):
            q2 = stacked_q(hp)

            def cond(st):
                j, carry, _ = st
                return jnp.logical_and(j >= 0, jnp.min(carry) < EXIT_LOG2)

            def body(st):
                j, carry, acc = st
                start = pl.multiple_of(j * tb, tb)
                z = scores(hp, q2, start)
                cum = jnp.dot(_softplus2(z).astype(BF16), tri, preferred_element_type=F32)
                w = jnp.exp2(z - cum - carry)
                return j - 1, carry + cum[:, 0:1], acc + weighted_v(hp, w, start)

            init = (qi - 2, carry_scr[hp], jnp.zeros((2 * tb, LANES), F32))
            acc = lax.while_loop(cond, body, init)[2]
            o_ref[hp] = (acc_scr[hp] + unstack(acc)).astype(o_ref.dtype)
            return 0

        lax.fori_loop(0, N_HEAD_PAIRS, far, 0)


def _attention(q, k, v, *, nb, tb):
    nhp, t, _ = q.shape
    l = t // nb
    nq = l // tb
    tri = (lax.broadcasted_iota(jnp.int32, (tb, tb), 0) >= lax.broadcasted_iota(jnp.int32, (tb, tb), 1)).astype(BF16)
    qspec = pl.BlockSpec((nhp, tb, LANES), lambda b, i: (0, b * nq + i, 0))
    kvspec = pl.BlockSpec((nhp, l, LANES), lambda b, i: (0, b, 0))
    return pl.pallas_call(
        functools.partial(_attn_kernel, tb=tb),
        grid=(nb, nq),
        in_specs=[qspec, kvspec, kvspec, pl.BlockSpec((tb, tb), lambda b, i: (0, 0))],
        out_specs=qspec,
        out_shape=jax.ShapeDtypeStruct(q.shape, BF16),
        scratch_shapes=[pltpu.VMEM((nhp, tb, LANES), F32), pltpu.VMEM((nhp, 2 * tb, 1), F32)],
        compiler_params=pltpu.CompilerParams(
            dimension_semantics=("arbitrary", "arbitrary"), vmem_limit_bytes=VMEM_LIMIT),
        name="attention",
    )(q, k, v, tri)


def _out_kernel(x_ref, yg_ref, zs_ref, ya_ref, za_ref, gs_ref, ga_ref,
                wglu_ref, bglu_ref, wbs_ref, wba_ref, wout_ref, g_ref, o_ref):
    gl = jnp.dot(yg_ref[...], wglu_ref[...].astype(BF16), preferred_element_type=F32) + bglu_ref[...]
    ys = gl[:, :SSM_WIDTH] * _sigmoid(gl[:, SSM_WIDTH:]) * zs_ref[...].astype(F32)
    ya = jnp.concatenate([ya_ref[hp] for hp in range(N_HEAD_PAIRS)], axis=1).astype(F32) * za_ref[...].astype(F32)
    merged = (gs_ref[...].astype(F32) * jnp.dot(ys.astype(BF16), wbs_ref[...].astype(BF16), preferred_element_type=F32)
              + ga_ref[...].astype(F32) * jnp.dot(ya.astype(BF16), wba_ref[...].astype(BF16), preferred_element_type=F32))
    out = jnp.dot(merged.astype(BF16), wout_ref[...].astype(BF16), preferred_element_type=F32)
    ms = jnp.mean(out * out, axis=-1, keepdims=True)
    o_ref[...] = x_ref[...] + out * lax.rsqrt(ms + EPS) * g_ref[...]


def _out_block(x, yg, zs, ya, za, gs, ga, wglu, bglu, wbs, wba, wout, g, li, *, tm):
    t = x.shape[0]
    npb = yg.shape[1] // tm
    row = lambda n: pl.BlockSpec((tm, n), lambda i: (i, 0))
    seq = pl.BlockSpec((None, tm, SSM_WIDTH), lambda i: (i // npb, i % npb, 0))
    full = lambda a: _layer_spec(a, li)
    weights = (wglu, bglu, wbs, wba, wout, g)
    return pl.pallas_call(
        _out_kernel,
        grid=(t // tm,),
        in_specs=[row(D_MODEL), seq, row(512),
                  pl.BlockSpec((N_HEAD_PAIRS, tm, LANES), lambda i: (0, i, 0)),
                  row(512), row(D_MODEL), row(D_MODEL)] + [full(a) for a in weights],
        out_specs=row(D_MODEL),
        out_shape=jax.ShapeDtypeStruct(x.shape, F32),
        compiler_params=pltpu.CompilerParams(
            dimension_semantics=("arbitrary",), vmem_limit_bytes=VMEM_LIMIT),
        name="out_block",
    )(x, yg, zs, ya, za, gs, ga, *weights)


def _tile(n, target):
    t = min(n, target)
    assert n % t == 0, (n, target)
    return t


def kernel(x, pre_norm_g, post_norm_g, w_in, ssm_a_re, ssm_a_im, ssm_log_dt, ssm_b_re, ssm_b_im,
           ssm_c_re, ssm_c_im, ssm_d, w_glu, b_glu, w_branch_ssm, w_branch_attn, w_out):
    nb, l, d = x.shape
    depth = w_in.shape[0]
    assert d == D_MODEL and w_in.shape[2] == IN_COLS
    t = nb * l
    tm = _tile(l, 512)
    lc = _tile(l, 256)
    tb = _tile(l, 256)

    abar_re, abar_im, bb_re, bb_im = _ssm_discretise(
        ssm_a_re, ssm_a_im, ssm_log_dt[..., None],
        ssm_b_re.transpose(0, 3, 1, 2), ssm_b_im.transpose(0, 3, 1, 2))
    bb_re = bb_re.transpose(0, 2, 1, 3)
    bb_im = bb_im.transpose(0, 2, 1, 3)
    wb = jnp.concatenate([_block_diag_slabs(bb_re), _block_diag_slabs(bb_im)], axis=3).astype(BF16)
    wc = jnp.concatenate([_block_diag_slabs(ssm_c_re.transpose(0, 1, 3, 2)),
                          _block_diag_slabs(-ssm_c_im.transpose(0, 1, 3, 2))], axis=2).astype(BF16)
    ar = abar_re.reshape(depth, 2, SUBLANES, LANES)
    ai = abar_im.reshape(depth, 2, SUBLANES, LANES)

    row = lambda a: a[:, None, :]
    pre_g, post_g, d_skip, b_glu_r = row(pre_norm_g), row(post_norm_g), row(ssm_d), row(b_glu)

    xt = x.reshape(t, d)
    for li in range(depth):
        u, zs, q, k, v, za, gs, ga = _in_proj(xt, pre_g, w_in, li, nb=nb, tm=tm)
        yg = _ssm(u, wb, ar, ai, wc, d_skip, li, lc=lc)
        ya = _attention(q, k, v, nb=nb, tb=tb)
        xt = _out_block(xt, yg, zs, ya, za, gs, ga,
                        w_glu, b_glu_r, w_branch_ssm, w_branch_attn, w_out, post_g, li, tm=tm)
    return xt.reshape(nb, l, d)
```

```python
import functools
import math

import jax
import jax.numpy as jnp
from jax import lax
from jax.experimental import pallas as pl
from jax.experimental.pallas import tpu as pltpu

F32 = jnp.float32
BF16 = jnp.bfloat16

D_MODEL = 1024
SSM_WIDTH = 512
SSM_GROUP = 16
SSM_GROUPS = 32
SSM_STATE = 64
ATTN_WIDTH = 512
HEAD_DIM = 64
EPS = 1e-6
IN_COLS = 5120

LANES = 128
SUBLANES = 8
N_SLABS = SSM_WIDTH // LANES
SLAB_STATE = (LANES // SSM_GROUP) * SSM_STATE
STATE_CHUNKS = SLAB_STATE // LANES
N_STATE_VREGS = 4
N_HEAD_PAIRS = ATTN_WIDTH // LANES

VMEM_LIMIT = 56 * 1024 * 1024

LOG2E = math.log2(math.e)
Q_SCALE = HEAD_DIM ** -0.5 * LOG2E
EXIT_LOG2 = 150.0
NO_BLOCK = 1e30


def _sigmoid(x):
    return 1.0 / (1.0 + jnp.exp(-x))


def _silu(x):
    return x * _sigmoid(x)


def _gelu_tanh(x):
    return 0.5 * x * (1.0 + jnp.tanh(math.sqrt(2.0 / math.pi) * (x + 0.044715 * (x * x * x))))


def _in_proj_kernel(x_ref, g_ref, w_ref, u_ref, zs_ref, q_ref, k_ref, v_ref, za_ref, gs_ref, ga_ref):
    x = x_ref[...]
    ms = jnp.mean(x * x, axis=-1, keepdims=True)
    h = (x * lax.rsqrt(ms + EPS) * g_ref[...]).astype(BF16)

    def proj(lo, hi):
        return jnp.dot(h, w_ref[:, lo:hi].astype(BF16), preferred_element_type=F32)

    gs_ref[...] = _sigmoid(proj(3072, 4096)).astype(BF16)
    ga_ref[...] = _sigmoid(proj(4096, 5120)).astype(BF16)
    zs_ref[...] = _silu(proj(512, 1024)).astype(BF16)
    za_ref[...] = _silu(proj(2560, 3072)).astype(BF16)
    for ref, lo, scale in ((q_ref, 1024, Q_SCALE), (k_ref, 1536, None), (v_ref, 2048, None)):
        y = proj(lo, lo + ATTN_WIDTH)
        if scale is not None:
            y = y * scale
        for hp in range(N_HEAD_PAIRS):
            ref[hp] = y[:, hp * LANES:(hp + 1) * LANES].astype(BF16)
    u_ref[...] = proj(0, 512).astype(BF16)


def _layer_spec(a, li):
    return pl.BlockSpec((None,) + a.shape[1:], lambda *_: (li,) + (0,) * (a.ndim - 1),
                        pipeline_mode=pl.Buffered(1))


def _in_proj(x, g, w, li, *, nb, tm):
    t = x.shape[0]
    npb = t // nb // tm
    row = lambda n: pl.BlockSpec((tm, n), lambda i: (i, 0))
    seq = pl.BlockSpec((None, tm, SSM_WIDTH), lambda i: (i // npb, i % npb, 0))
    full = lambda a: _layer_spec(a, li)
    heads = pl.BlockSpec((N_HEAD_PAIRS, tm, LANES), lambda i: (0, i, 0))
    heads_shape = jax.ShapeDtypeStruct((N_HEAD_PAIRS, t, LANES), BF16)
    flat = lambda n: jax.ShapeDtypeStruct((t, n), BF16)
    return pl.pallas_call(
        _in_proj_kernel,
        grid=(t // tm,),
        in_specs=[row(D_MODEL), full(g), full(w)],
        out_specs=[seq, row(512), heads, heads, heads, row(512), row(1024), row(1024)],
        out_shape=[jax.ShapeDtypeStruct((nb, t // nb, SSM_WIDTH), BF16), flat(512),
                   heads_shape, heads_shape, heads_shape, flat(512), flat(1024), flat(1024)],
        compiler_params=pltpu.CompilerParams(
            dimension_semantics=("arbitrary",), vmem_limit_bytes=VMEM_LIMIT),
        name="in_proj",
    )(x, g, w)


def _ssm_param_kernel(are_ref, aim_ref, ldt_ref, bre_ref, bim_ref, cre_ref, cim_ref,
                      abr_ref, abi_ref, wb_ref, wc_ref):
    a_re = are_ref[0]
    a_im = aim_ref[0]
    dt = jnp.exp(ldt_ref[0])
    mag = jnp.exp(a_re * dt)
    abar_re = mag * jnp.cos(a_im * dt)
    abar_im = mag * jnp.sin(a_im * dt)
    nr = abar_re - 1.0
    ni = abar_im
    den = a_re * a_re + a_im * a_im
    f_re = ((nr * a_re + ni * a_im) / den)[:, None, :]
    f_im = ((ni * a_re - nr * a_im) / den)[:, None, :]
    abr_ref[0] = abar_re
    abi_ref[0] = abar_im
    b_re = bre_ref[0]
    b_im = bim_ref[0]
    bb_re = f_re * b_re - f_im * b_im
    bb_im = f_re * b_im + f_im * b_re

    gps = LANES // SSM_GROUP
    iota = lambda shape, dim: lax.broadcasted_iota(jnp.int32, shape, dim)
    spread = (iota((SSM_STATE, SLAB_STATE), 1) % SSM_STATE == iota((SSM_STATE, SLAB_STATE), 0)).astype(BF16)
    stack = (iota((SLAB_STATE, SSM_STATE), 0) % SSM_STATE == iota((SLAB_STATE, SSM_STATE), 1)).astype(BF16)
    keep_b = iota((LANES, SLAB_STATE), 0) // SSM_GROUP == iota((LANES, SLAB_STATE), 1) // SSM_STATE
    keep_c = iota((SLAB_STATE, LANES), 0) // SSM_STATE == iota((SLAB_STATE, LANES), 1) // SSM_GROUP

    def slab_rows(w, s):
        return w[s * gps:(s + 1) * gps].reshape(LANES, SSM_STATE).astype(BF16)

    for s in range(N_SLABS):
        for part, w in enumerate((bb_re, bb_im)):
            wide = jnp.dot(slab_rows(w, s), spread, preferred_element_type=F32)
            wb_ref[0, s, :, part * SLAB_STATE:(part + 1) * SLAB_STATE] = jnp.where(keep_b, wide, 0.0).astype(BF16)
        for part, (w, sign) in enumerate(((cre_ref[0], 1.0), (cim_ref[0], -1.0))):
            tall = lax.dot_general(stack, slab_rows(w, s), (((1,), (1,)), ((), ())),
                                   preferred_element_type=F32)
            wc_ref[0, s, part * SLAB_STATE:(part + 1) * SLAB_STATE, :] = jnp.where(keep_c, sign * tall, 0.0).astype(BF16)


def _ssm_params(a_re, a_im, log_dt, b_re_t, b_im_t, c_re, c_im):
    depth = a_re.shape[0]
    spec = lambda shape: pl.BlockSpec((1,) + shape[1:], lambda l: (l,) + (0,) * (len(shape) - 1))
    ins = (a_re, a_im, log_dt, b_re_t, b_im_t, c_re, c_im)
    out_shapes = [jax.ShapeDtypeStruct(a_re.shape, F32), jax.ShapeDtypeStruct(a_re.shape, F32),
                  jax.ShapeDtypeStruct((depth, N_SLABS, LANES, 2 * SLAB_STATE), BF16),
                  jax.ShapeDtypeStruct((depth, N_SLABS, 2 * SLAB_STATE, LANES), BF16)]
    return pl.pallas_call(
        _ssm_param_kernel,
        grid=(depth,),
        in_specs=[spec(a.shape) for a in ins],
        out_specs=[spec(o.shape) for o in out_shapes],
        out_shape=out_shapes,
        name="ssm_params",
    )(*ins)


def _ssm_kernel(u_ref, wb_ref, ar_ref, ai_ref, wc_ref, d_ref, y_ref, bu_scr, x_scr, st_scr, *, nb, lc):
    @pl.when(pl.program_id(0) == 0)
    def _():
        st_scr[...] = jnp.zeros_like(st_scr)

    nt = lc // SUBLANES

    for b in range(nb):
        for s in range(N_SLABS):
            bu = jnp.dot(u_ref[b, :, s * LANES:(s + 1) * LANES], wb_ref[s], preferred_element_type=F32)
            vp, base = s // 2, (s % 2) * STATE_CHUNKS
            for c in range(STATE_CHUNKS):
                j = base + c
                re = bu[:, c * LANES:(c + 1) * LANES]
                im = bu[:, SLAB_STATE + c * LANES:SLAB_STATE + (c + 1) * LANES]
                bu_scr[b, 2 * vp, :, j * SUBLANES:(j + 1) * SUBLANES, :] = re.reshape(nt, SUBLANES, LANES)
                bu_scr[b, 2 * vp + 1, :, j * SUBLANES:(j + 1) * SUBLANES, :] = im.reshape(nt, SUBLANES, LANES)

    a_r = [ar_ref[0], ar_ref[1]]
    a_i = [ai_ref[0], ai_ref[1]]

    def steps(i, carry):
        xs = list(carry)
        for r in range(SUBLANES):
            row = pl.multiple_of(i * (SUBLANES * SUBLANES) + r * SUBLANES, SUBLANES)
            for b in range(nb):
                for vp in range(2):
                    k = (b * 2 + vp) * 2
                    b_r = bu_scr[b, 2 * vp, i, pl.ds(r, SUBLANES, stride=SUBLANES), :]
                    b_i = bu_scr[b, 2 * vp + 1, i, pl.ds(r, SUBLANES, stride=SUBLANES), :]
                    x_r, x_i = xs[k], xs[k + 1]
                    n_r = a_r[vp] * x_r - a_i[vp] * x_i + b_r
                    n_i = a_r[vp] * x_i + a_i[vp] * x_r + b_i
                    x_scr[b, 2 * vp, pl.ds(row, SUBLANES), :] = n_r
                    x_scr[b, 2 * vp + 1, pl.ds(row, SUBLANES), :] = n_i
                    xs[k], xs[k + 1] = n_r, n_i
        return tuple(xs)

    init = tuple(st_scr[b, v] for b in range(nb) for v in range(N_STATE_VREGS))
    final = lax.fori_loop(0, nt, steps, init)
    for b in range(nb):
        for v in range(N_STATE_VREGS):
            st_scr[b, v] = final[b * N_STATE_VREGS + v]

    for b in range(nb):
        for s in range(N_SLABS):
            vp, base = s // 2, (s % 2) * STATE_CHUNKS
            parts = [x_scr[b, 2 * vp + ri, pl.ds(base + c, lc, stride=SUBLANES), :]
                     for ri in range(2) for c in range(STATE_CHUNKS)]
            xs = jnp.concatenate(parts, axis=1).astype(BF16)
            y = jnp.dot(xs, wc_ref[s], preferred_element_type=F32)
            sl = slice(s * LANES, (s + 1) * LANES)
            y = y + d_ref[:, sl] * u_ref[b, :, sl].astype(F32)
            y_ref[b, :, sl] = _gelu_tanh(y).astype(BF16)


def _ssm(u, wb, ar, ai, wc, d, li, *, lc):
    nb, l, _ = u.shape
    full = lambda a: _layer_spec(a, li)
    blk = pl.BlockSpec((nb, lc, SSM_WIDTH), lambda c: (0, c, 0))
    return pl.pallas_call(
        functools.partial(_ssm_kernel, nb=nb, lc=lc),
        grid=(l // lc,),
        in_specs=[blk, full(wb), full(ar), full(ai), full(wc), full(d)],
        out_specs=blk,
        out_shape=jax.ShapeDtypeStruct(u.shape, BF16),
        scratch_shapes=[
            pltpu.VMEM((nb, N_STATE_VREGS, lc // SUBLANES, SUBLANES * SUBLANES, LANES), F32),
            pltpu.VMEM((nb, N_STATE_VREGS, lc * SUBLANES, LANES), F32),
            pltpu.VMEM((nb, N_STATE_VREGS, SUBLANES, LANES), F32),
        ],
        compiler_params=pltpu.CompilerParams(
            dimension_semantics=("arbitrary",), vmem_limit_bytes=VMEM_LIMIT),
        name="ssm",
    )(u, wb, ar, ai, wc, d)


def _softplus2(z):
    return jnp.maximum(z, 0.0) + jnp.log2(1.0 + jnp.exp2(-jnp.abs(z)))


def _attn_kernel(q_ref, k_ref, v_ref, tri_ref, o_ref, acc_scr, carry_scr, *, tb):
    qi = pl.program_id(1)
    low = lax.broadcasted_iota(jnp.int32, (tb, LANES), 1) < HEAD_DIM
    causal = (lax.broadcasted_iota(jnp.int32, (tb, tb), 1)
              < lax.broadcasted_iota(jnp.int32, (tb, tb), 0))
    tri = tri_ref[...]
    start0 = pl.multiple_of(qi * tb, tb)
    start1 = pl.multiple_of(jnp.maximum(qi - 1, 0) * tb, tb)
    no_prev = jnp.where(qi > 0, 0.0, NO_BLOCK)

    causal2 = jnp.concatenate([causal, causal], axis=0)

    def stacked_q(hp):
        q = q_ref[hp]
        zero = jnp.zeros_like(q)
        return jnp.concatenate([jnp.where(low, q, zero), jnp.where(low, zero, q)], axis=0)

    def scores(hp, q2, start):
        return lax.dot_general(q2, k_ref[hp, pl.ds(start, tb), :], (((1,), (1,)), ((), ())),
                               preferred_element_type=F32)

    def weighted_v(hp, w, start):
        return jnp.dot(w.astype(BF16), v_ref[hp, pl.ds(start, tb), :], preferred_element_type=F32)

    def unstack(x):
        return jnp.where(low, x[:tb], x[tb:])

    def stage_a(hp):
        q2 = stacked_q(hp)
        z0 = scores(hp, q2, start0)
        z1 = scores(hp, q2, start1)
        sp = jnp.concatenate([jnp.where(causal2, _softplus2(z0), 0.0), _softplus2(z1)], axis=0)
        return z0, z1, sp.astype(BF16)

    def stage_b(hp, z0, z1, sp):
        cum = jnp.dot(sp, tri, preferred_element_type=F32)
        cum0, cum1 = cum[:2 * tb], cum[2 * tb:]
        c0 = cum0[:, 0:1]
        w0 = jnp.where(causal2, jnp.exp2(z0 - cum0), 0.0).astype(BF16)
        w1 = jnp.exp2(z1 - cum1 - (c0 + no_prev)).astype(BF16)
        return w0, w1, c0 + cum1[:, 0:1]

    def stage_c(hp, w0, w1, carry, min_carry):
        out = unstack(jnp.dot(w0, v_ref[hp, pl.ds(start0, tb), :], preferred_element_type=F32)
                      + jnp.dot(w1, v_ref[hp, pl.ds(start1, tb), :], preferred_element_type=F32))
        carry_scr[hp] = carry
        acc_scr[hp] = out
        o_ref[hp] = out.astype(o_ref.dtype)
        return jnp.minimum(min_carry, jnp.minimum(carry[:tb], carry[tb:]))

    min_carry = jnp.full((tb, 1), jnp.inf, F32)
    a_out, b_out = {}, {}
    for step in range(N_HEAD_PAIRS + 2):
        if step < N_HEAD_PAIRS:
            a_out[step] = stage_a(step)
        if 0 <= step - 1 < N_HEAD_PAIRS:
            b_out[step - 1] = stage_b(step - 1, *a_out.pop(step - 1))
        if 0 <= step - 2 < N_HEAD_PAIRS:
            min_carry = stage_c(step - 2, *b_out.pop(step - 2), min_carry)

    @pl.when(jnp.logical_and(qi >= 2, jnp.min(min_carry) < EXIT_LOG2))
    def _():
        def far(hp, _):
            q2 = stacked_q(hp)

            def cond(st):
                j, carry, _ = st
                return jnp.logical_and(j >= 0, jnp.min(carry) < EXIT_LOG2)

            def body(st):
                j, carry, acc = st
                start = pl.multiple_of(j * tb, tb)
                z = scores(hp, q2, start)
                cum = jnp.dot(_softplus2(z).astype(BF16), tri, preferred_element_type=F32)
                w = jnp.exp2(z - cum - carry)
                return j - 1, carry + cum[:, 0:1], acc + weighted_v(hp, w, start)

            init = (qi - 2, carry_scr[hp], jnp.zeros((2 * tb, LANES), F32))
            acc = lax.while_loop(cond, body, init)[2]
            o_ref[hp] = (acc_scr[hp] + unstack(acc)).astype(o_ref.dtype)
            return 0

        lax.fori_loop(0, N_HEAD_PAIRS, far, 0)


def _attention(q, k, v, *, nb, tb):
    nhp, t, _ = q.shape
    l = t // nb
    nq = l // tb
    tri = (lax.broadcasted_iota(jnp.int32, (tb, tb), 0) >= lax.broadcasted_iota(jnp.int32, (tb, tb), 1)).astype(BF16)
    qspec = pl.BlockSpec((nhp, tb, LANES), lambda b, i: (0, b * nq + i, 0))
    kvspec = pl.BlockSpec((nhp, l, LANES), lambda b, i: (0, b, 0))
    return pl.pallas_call(
        functools.partial(_attn_kernel, tb=tb),
        grid=(nb, nq),
        in_specs=[qspec, kvspec, kvspec, pl.BlockSpec((tb, tb), lambda b, i: (0, 0))],
        out_specs=qspec,
        out_shape=jax.ShapeDtypeStruct(q.shape, BF16),
        scratch_shapes=[pltpu.VMEM((nhp, tb, LANES), F32), pltpu.VMEM((nhp, 2 * tb, 1), F32)],
        compiler_params=pltpu.CompilerParams(
            dimension_semantics=("arbitrary", "arbitrary"), vmem_limit_bytes=VMEM_LIMIT),
        name="attention",
    )(q, k, v, tri)


def _out_kernel(x_ref, yg_ref, zs_ref, ya_ref, za_ref, gs_ref, ga_ref,
                wglu_ref, bglu_ref, wbs_ref, wba_ref, wout_ref, g_ref, o_ref):
    gl = jnp.dot(yg_ref[...], wglu_ref[...].astype(BF16), preferred_element_type=F32) + bglu_ref[...]
    ys = gl[:, :SSM_WIDTH] * _sigmoid(gl[:, SSM_WIDTH:]) * zs_ref[...].astype(F32)
    ya = jnp.concatenate([ya_ref[hp] for hp in range(N_HEAD_PAIRS)], axis=1).astype(F32) * za_ref[...].astype(F32)
    merged = (gs_ref[...].astype(F32) * jnp.dot(ys.astype(BF16), wbs_ref[...].astype(BF16), preferred_element_type=F32)
              + ga_ref[...].astype(F32) * jnp.dot(ya.astype(BF16), wba_ref[...].astype(BF16), preferred_element_type=F32))
    out = jnp.dot(merged.astype(BF16), wout_ref[...].astype(BF16), preferred_element_type=F32)
    ms = jnp.mean(out * out, axis=-1, keepdims=True)
    o_ref[...] = x_ref[...] + out * lax.rsqrt(ms + EPS) * g_ref[...]


def _out_block(x, yg, zs, ya, za, gs, ga, wglu, bglu, wbs, wba, wout, g, li, *, tm):
    t = x.shape[0]
    npb = yg.shape[1] // tm
    row = lambda n: pl.BlockSpec((tm, n), lambda i: (i, 0))
    seq = pl.BlockSpec((None, tm, SSM_WIDTH), lambda i: (i // npb, i % npb, 0))
    full = lambda a: _layer_spec(a, li)
    weights = (wglu, bglu, wbs, wba, wout, g)
    return pl.pallas_call(
        _out_kernel,
        grid=(t // tm,),
        in_specs=[row(D_MODEL), seq, row(512),
                  pl.BlockSpec((N_HEAD_PAIRS, tm, LANES), lambda i: (0, i, 0)),
                  row(512), row(D_MODEL), row(D_MODEL)] + [full(a) for a in weights],
        out_specs=row(D_MODEL),
        out_shape=jax.ShapeDtypeStruct(x.shape, F32),
        compiler_params=pltpu.CompilerParams(
            dimension_semantics=("arbitrary",), vmem_limit_bytes=VMEM_LIMIT),
        name="out_block",
    )(x, yg, zs, ya, za, gs, ga, *weights)


def _tile(n, target):
    t = min(n, target)
    assert n % t == 0, (n, target)
    return t


def kernel(x, pre_norm_g, post_norm_g, w_in, ssm_a_re, ssm_a_im, ssm_log_dt, ssm_b_re, ssm_b_im,
           ssm_c_re, ssm_c_im, ssm_d, w_glu, b_glu, w_branch_ssm, w_branch_attn, w_out):
    nb, l, d = x.shape
    depth = w_in.shape[0]
    assert d == D_MODEL and w_in.shape[2] == IN_COLS
    t = nb * l
    tm = _tile(l, 1024)
    lc = _tile(l, 512)
    tb = _tile(l, 256)

    abar_re, abar_im, wb, wc = _ssm_params(
        ssm_a_re, ssm_a_im, ssm_log_dt[..., None],
        ssm_b_re.transpose(0, 1, 3, 2), ssm_b_im.transpose(0, 1, 3, 2),
        ssm_c_re, ssm_c_im)
    ar = abar_re.reshape(depth, 2, SUBLANES, LANES)
    ai = abar_im.reshape(depth, 2, SUBLANES, LANES)

    row = lambda a: a[:, None, :]
    pre_g, post_g, d_skip, b_glu_r = row(pre_norm_g), row(post_norm_g), row(ssm_d), row(b_glu)

    xt = x.reshape(t, d)
    for li in range(depth):
        u, zs, q, k, v, za, gs, ga = _in_proj(xt, pre_g, w_in, li, nb=nb, tm=tm)
        yg = _ssm(u, wb, ar, ai, wc, d_skip, li, lc=lc)
        ya = _attention(q, k, v, nb=nb, tb=tb)
        xt = _out_block(xt, yg, zs, ya, za, gs, ga,
                        w_glu, b_glu_r, w_branch_ssm, w_branch_attn, w_out, post_g, li, tm=tm)
    return xt.reshape(nb, l, d)
```

```python
import functools
import math

import jax
import jax.numpy as jnp
from jax import lax
from jax.experimental import pallas as pl
from jax.experimental.pallas import tpu as pltpu

F32 = jnp.float32
BF16 = jnp.bfloat16

D_MODEL = 1024
SSM_WIDTH = 512
SSM_GROUP = 16
SSM_GROUPS = 32
SSM_STATE = 64
ATTN_WIDTH = 512
HEAD_DIM = 64
EPS = 1e-6
IN_COLS = 5120

LANES = 128
SUBLANES = 8
N_SLABS = SSM_WIDTH // LANES
SLAB_STATE = (LANES // SSM_GROUP) * SSM_STATE
STATE_CHUNKS = SLAB_STATE // LANES
N_STATE_VREGS = 4
N_HEAD_PAIRS = ATTN_WIDTH // LANES

VMEM_LIMIT = 56 * 1024 * 1024

LOG2E = math.log2(math.e)
Q_SCALE = HEAD_DIM ** -0.5 * LOG2E
EXIT_LOG2 = 150.0
NO_BLOCK = 1e30


def _sigmoid(x):
    return 1.0 / (1.0 + jnp.exp(-x))


def _silu(x):
    return x * _sigmoid(x)


def _gelu_tanh(x):
    return 0.5 * x * (1.0 + jnp.tanh(math.sqrt(2.0 / math.pi) * (x + 0.044715 * (x * x * x))))


def _in_proj_kernel(x_ref, g_ref, w_ref, u_ref, zs_ref, q_ref, k_ref, v_ref, za_ref, gs_ref, ga_ref):
    x = x_ref[...]
    ms = jnp.mean(x * x, axis=-1, keepdims=True)
    h = (x * lax.rsqrt(ms + EPS) * g_ref[...]).astype(BF16)

    def proj(lo, hi):
        return jnp.dot(h, w_ref[:, lo:hi].astype(BF16), preferred_element_type=F32)

    gs_ref[...] = _sigmoid(proj(3072, 4096)).astype(BF16)
    ga_ref[...] = _sigmoid(proj(4096, 5120)).astype(BF16)
    zs_ref[...] = _silu(proj(512, 1024)).astype(BF16)
    za_ref[...] = _silu(proj(2560, 3072)).astype(BF16)
    for ref, lo, scale in ((q_ref, 1024, Q_SCALE), (k_ref, 1536, None), (v_ref, 2048, None)):
        y = proj(lo, lo + ATTN_WIDTH)
        if scale is not None:
            y = y * scale
        for hp in range(N_HEAD_PAIRS):
            ref[hp] = y[:, hp * LANES:(hp + 1) * LANES].astype(BF16)
    u_ref[...] = proj(0, 512).astype(BF16)


def _layer_spec(a, li):
    return pl.BlockSpec((None,) + a.shape[1:], lambda *_: (li,) + (0,) * (a.ndim - 1),
                        pipeline_mode=pl.Buffered(1))


def _in_proj(x, g, w, li, *, nb, tm):
    t = x.shape[0]
    npb = t // nb // tm
    row = lambda n: pl.BlockSpec((tm, n), lambda i: (i, 0))
    seq = pl.BlockSpec((None, tm, SSM_WIDTH), lambda i: (i // npb, i % npb, 0))
    full = lambda a: _layer_spec(a, li)
    heads = pl.BlockSpec((N_HEAD_PAIRS, tm, LANES), lambda i: (0, i, 0))
    heads_shape = jax.ShapeDtypeStruct((N_HEAD_PAIRS, t, LANES), BF16)
    flat = lambda n: jax.ShapeDtypeStruct((t, n), BF16)
    return pl.pallas_call(
        _in_proj_kernel,
        grid=(t // tm,),
        in_specs=[row(D_MODEL), full(g), full(w)],
        out_specs=[seq, row(512), heads, heads, heads, row(512), row(1024), row(1024)],
        out_shape=[jax.ShapeDtypeStruct((nb, t // nb, SSM_WIDTH), BF16), flat(512),
                   heads_shape, heads_shape, heads_shape, flat(512), flat(1024), flat(1024)],
        compiler_params=pltpu.CompilerParams(
            dimension_semantics=("arbitrary",), vmem_limit_bytes=VMEM_LIMIT),
        name="in_proj",
    )(x, g, w)


def _ssm_param_kernel(are_ref, aim_ref, ldt_ref, bre_ref, bim_ref, cre_ref, cim_ref,
                      abr_ref, abi_ref, wb_ref, wc_ref):
    a_re = are_ref[0]
    a_im = aim_ref[0]
    dt = jnp.exp(ldt_ref[0])
    mag = jnp.exp(a_re * dt)
    abar_re = mag * jnp.cos(a_im * dt)
    abar_im = mag * jnp.sin(a_im * dt)
    nr = abar_re - 1.0
    ni = abar_im
    den = a_re * a_re + a_im * a_im
    f_re = ((nr * a_re + ni * a_im) / den)[:, None, :]
    f_im = ((ni * a_re - nr * a_im) / den)[:, None, :]
    abr_ref[0] = abar_re
    abi_ref[0] = abar_im
    b_re = bre_ref[0]
    b_im = bim_ref[0]
    bb_re = f_re * b_re - f_im * b_im
    bb_im = f_re * b_im + f_im * b_re

    gps = LANES // SSM_GROUP
    iota = lambda shape, dim: lax.broadcasted_iota(jnp.int32, shape, dim)
    spread = (iota((SSM_STATE, SLAB_STATE), 1) % SSM_STATE == iota((SSM_STATE, SLAB_STATE), 0)).astype(BF16)
    stack = (iota((SLAB_STATE, SSM_STATE), 0) % SSM_STATE == iota((SLAB_STATE, SSM_STATE), 1)).astype(BF16)
    keep_b = iota((LANES, SLAB_STATE), 0) // SSM_GROUP == iota((LANES, SLAB_STATE), 1) // SSM_STATE
    keep_c = iota((SLAB_STATE, LANES), 0) // SSM_STATE == iota((SLAB_STATE, LANES), 1) // SSM_GROUP

    def slab_rows(w, s):
        return w[s * gps:(s + 1) * gps].reshape(LANES, SSM_STATE).astype(BF16)

    for s in range(N_SLABS):
        for part, w in enumerate((bb_re, bb_im)):
            wide = jnp.dot(slab_rows(w, s), spread, preferred_element_type=F32)
            wb_ref[0, s, :, part * SLAB_STATE:(part + 1) * SLAB_STATE] = jnp.where(keep_b, wide, 0.0).astype(BF16)
        for part, (w, sign) in enumerate(((cre_ref[0], 1.0), (cim_ref[0], -1.0))):
            tall = lax.dot_general(stack, slab_rows(w, s), (((1,), (1,)), ((), ())),
                                   preferred_element_type=F32)
            wc_ref[0, s, part * SLAB_STATE:(part + 1) * SLAB_STATE, :] = jnp.where(keep_c, sign * tall, 0.0).astype(BF16)


def _ssm_params(a_re, a_im, log_dt, b_re_t, b_im_t, c_re, c_im):
    depth = a_re.shape[0]
    spec = lambda shape: pl.BlockSpec((1,) + shape[1:], lambda l: (l,) + (0,) * (len(shape) - 1))
    ins = (a_re, a_im, log_dt, b_re_t, b_im_t, c_re, c_im)
    out_shapes = [jax.ShapeDtypeStruct(a_re.shape, F32), jax.ShapeDtypeStruct(a_re.shape, F32),
                  jax.ShapeDtypeStruct((depth, N_SLABS, LANES, 2 * SLAB_STATE), BF16),
                  jax.ShapeDtypeStruct((depth, N_SLABS, 2 * SLAB_STATE, LANES), BF16)]
    return pl.pallas_call(
        _ssm_param_kernel,
        grid=(depth,),
        in_specs=[spec(a.shape) for a in ins],
        out_specs=[spec(o.shape) for o in out_shapes],
        out_shape=out_shapes,
        name="ssm_params",
    )(*ins)


def _ssm_kernel(u_ref, wb_ref, ar_ref, ai_ref, wc_ref, d_ref, y_ref, bu_scr, x_scr, st_scr, *, nb, lc):
    @pl.when(pl.program_id(0) == 0)
    def _():
        st_scr[...] = jnp.zeros_like(st_scr)

    nt = lc // SUBLANES

    for b in range(nb):
        for s in range(N_SLABS):
            bu = jnp.dot(u_ref[b, :, s * LANES:(s + 1) * LANES], wb_ref[s], preferred_element_type=F32)
            vp, base = s // 2, (s % 2) * STATE_CHUNKS
            for c in range(STATE_CHUNKS):
                j = base + c
                re = bu[:, c * LANES:(c + 1) * LANES]
                im = bu[:, SLAB_STATE + c * LANES:SLAB_STATE + (c + 1) * LANES]
                bu_scr[b, 2 * vp, :, j * SUBLANES:(j + 1) * SUBLANES, :] = re.reshape(nt, SUBLANES, LANES)
                bu_scr[b, 2 * vp + 1, :, j * SUBLANES:(j + 1) * SUBLANES, :] = im.reshape(nt, SUBLANES, LANES)

    a_r = [ar_ref[0], ar_ref[1]]
    a_i = [ai_ref[0], ai_ref[1]]

    def steps(i, carry):
        xs = list(carry)
        for r in range(SUBLANES):
            row = pl.multiple_of(i * (SUBLANES * SUBLANES) + r * SUBLANES, SUBLANES)
            for b in range(nb):
                for vp in range(2):
                    k = (b * 2 + vp) * 2
                    b_r = bu_scr[b, 2 * vp, i, pl.ds(r, SUBLANES, stride=SUBLANES), :]
                    b_i = bu_scr[b, 2 * vp + 1, i, pl.ds(r, SUBLANES, stride=SUBLANES), :]
                    x_r, x_i = xs[k], xs[k + 1]
                    n_r = a_r[vp] * x_r - a_i[vp] * x_i + b_r
                    n_i = a_r[vp] * x_i + a_i[vp] * x_r + b_i
                    x_scr[b, 2 * vp, pl.ds(row, SUBLANES), :] = n_r
                    x_scr[b, 2 * vp + 1, pl.ds(row, SUBLANES), :] = n_i
                    xs[k], xs[k + 1] = n_r, n_i
        return tuple(xs)

    init = tuple(st_scr[b, v] for b in range(nb) for v in range(N_STATE_VREGS))
    final = lax.fori_loop(0, nt, steps, init)
    for b in range(nb):
        for v in range(N_STATE_VREGS):
            st_scr[b, v] = final[b * N_STATE_VREGS + v]

    for b in range(nb):
        for s in range(N_SLABS):
            vp, base = s // 2, (s % 2) * STATE_CHUNKS
            parts = [x_scr[b, 2 * vp + ri, pl.ds(base + c, lc, stride=SUBLANES), :]
                     for ri in range(2) for c in range(STATE_CHUNKS)]
            xs = jnp.concatenate(parts, axis=1).astype(BF16)
            y = jnp.dot(xs, wc_ref[s], preferred_element_type=F32)
            sl = slice(s * LANES, (s + 1) * LANES)
            y = y + d_ref[:, sl] * u_ref[b, :, sl].astype(F32)
            y_ref[b, :, sl] = _gelu_tanh(y).astype(BF16)


def _ssm(u, wb, ar, ai, wc, d, li, *, lc):
    nb, l, _ = u.shape
    full = lambda a: _layer_spec(a, li)
    blk = pl.BlockSpec((nb, lc, SSM_WIDTH), lambda c: (0, c, 0))
    return pl.pallas_call(
        functools.partial(_ssm_kernel, nb=nb, lc=lc),
        grid=(l // lc,),
        in_specs=[blk, full(wb), full(ar), full(ai), full(wc), full(d)],
        out_specs=blk,
        out_shape=jax.ShapeDtypeStruct(u.shape, BF16),
        scratch_shapes=[
            pltpu.VMEM((nb, N_STATE_VREGS, lc // SUBLANES, SUBLANES * SUBLANES, LANES), F32),
            pltpu.VMEM((nb, N_STATE_VREGS, lc * SUBLANES, LANES), F32),
            pltpu.VMEM((nb, N_STATE_VREGS, SUBLANES, LANES), F32),
        ],
        compiler_params=pltpu.CompilerParams(
            dimension_semantics=("arbitrary",), vmem_limit_bytes=VMEM_LIMIT),
        name="ssm",
    )(u, wb, ar, ai, wc, d)


def _softplus2(z):
    return jnp.maximum(z, 0.0) + jnp.log2(1.0 + jnp.exp2(-jnp.abs(z)))


def _attn_kernel(q_ref, k_ref, v_ref, tri_ref, o_ref, acc_scr, carry_scr, *, tb):
    qi = pl.program_id(1)
    low = lax.broadcasted_iota(jnp.int32, (tb, LANES), 1) < HEAD_DIM
    causal = (lax.broadcasted_iota(jnp.int32, (tb, tb), 1)
              < lax.broadcasted_iota(jnp.int32, (tb, tb), 0))
    tri = tri_ref[...]
    start0 = pl.multiple_of(qi * tb, tb)
    start1 = pl.multiple_of(jnp.maximum(qi - 1, 0) * tb, tb)
    no_prev = jnp.where(qi > 0, 0.0, NO_BLOCK)

    causal2 = jnp.concatenate([causal, causal], axis=0)
    half = tb // 2

    def stacked_q(hp):
        q = q_ref[hp]
        zero = jnp.zeros_like(q)
        return jnp.concatenate([jnp.where(low, q, zero), jnp.where(low, zero, q)], axis=0)

    def scores(hp, q2, start):
        return lax.dot_general(q2, k_ref[hp, pl.ds(start, tb), :], (((1,), (1,)), ((), ())),
                               preferred_element_type=F32)

    def weighted_v(hp, w, start):
        return jnp.dot(w.astype(BF16), v_ref[hp, pl.ds(start, tb), :], preferred_element_type=F32)

    def unstack(x):
        return jnp.where(low, x[:tb], x[tb:])

    def near_prev(f, *tiles, dtype):
        rows = []
        for h in range(2):
            top = slice(h * tb, h * tb + half)
            bot = slice(h * tb + half, (h + 1) * tb)
            rows.append(f(*(t[top, :] for t in tiles)).astype(dtype))
            rows.append(jnp.concatenate([jnp.zeros((half, half), dtype),
                                         f(*(t[bot, half:] for t in tiles)).astype(dtype)], axis=1))
        return jnp.concatenate(rows, axis=0)

    def stage_a(hp):
        q2 = stacked_q(hp)
        z0 = scores(hp, q2, start0)
        z1 = scores(hp, q2, start1)
        sp = jnp.concatenate([jnp.where(causal2, _softplus2(z0), 0.0).astype(BF16),
                              near_prev(_softplus2, z1, dtype=BF16)], axis=0)
        return z0, z1, sp

    def stage_b(hp, z0, z1, sp):
        cum = jnp.dot(sp, tri, preferred_element_type=F32)
        cum0, cum1 = cum[:2 * tb], cum[2 * tb:]
        c0 = cum0[:, 0:1]
        w0 = jnp.where(causal2, jnp.exp2(z0 - cum0), 0.0).astype(BF16)
        c0p = jnp.broadcast_to(c0 + no_prev, (2 * tb, tb))
        w1 = near_prev(lambda z, c, p: jnp.exp2(z - c - p), z1, cum1, c0p, dtype=BF16)
        return w0, w1, c0 + cum1[:, 0:1]

    def stage_c(hp, w0, w1, carry, min_carry):
        out = unstack(jnp.dot(w0, v_ref[hp, pl.ds(start0, tb), :], preferred_element_type=F32)
                      + jnp.dot(w1, v_ref[hp, pl.ds(start1, tb), :], preferred_element_type=F32))
        carry_scr[hp] = carry
        acc_scr[hp] = out
        o_ref[hp] = out.astype(o_ref.dtype)
        return jnp.minimum(min_carry, jnp.minimum(carry[:tb], carry[tb:]))

    min_carry = jnp.full((tb, 1), jnp.inf, F32)
    a_out, b_out = {}, {}
    for step in range(N_HEAD_PAIRS + 2):
        if step < N_HEAD_PAIRS:
            a_out[step] = stage_a(step)
        if 0 <= step - 1 < N_HEAD_PAIRS:
            b_out[step - 1] = stage_b(step - 1, *a_out.pop(step - 1))
        if 0 <= step - 2 < N_HEAD_PAIRS:
            min_carry = stage_c(step - 2, *b_out.pop(step - 2), min_carry)

    @pl.when(jnp.logical_and(qi >= 1, jnp.min(min_carry) < EXIT_LOG2))
    def _():
        row_in_block = lax.broadcasted_iota(jnp.int32, (2 * tb, tb), 0) % tb
        skipped = jnp.logical_and(row_in_block >= half,
                                  lax.broadcasted_iota(jnp.int32, (2 * tb, tb), 1) < half)

        def far(hp, _):
            q2 = stacked_q(hp)

            z = scores(hp, q2, start1)
            sp = jnp.where(skipped, _softplus2(z), 0.0)
            cum = jnp.dot(sp.astype(BF16), tri, preferred_element_type=F32)
            carry = carry_scr[hp]
            w = jnp.where(skipped, jnp.exp2(z - cum - carry), 0.0)
            acc0 = weighted_v(hp, w, start1)
            carry0 = carry + cum[:, 0:1]

            def cond(st):
                j, carry, _ = st
                return jnp.logical_and(j >= 0, jnp.min(carry) < EXIT_LOG2)

            def body(st):
                j, carry, acc = st
                start = pl.multiple_of(j * tb, tb)
                z = scores(hp, q2, start)
                cum = jnp.dot(_softplus2(z).astype(BF16), tri, preferred_element_type=F32)
                w = jnp.exp2(z - cum - carry)
                return j - 1, carry + cum[:, 0:1], acc + weighted_v(hp, w, start)

            init = (qi - 2, carry0, acc0)
            acc = lax.while_loop(cond, body, init)[2]
            o_ref[hp] = (acc_scr[hp] + unstack(acc)).astype(o_ref.dtype)
            return 0

        lax.fori_loop(0, N_HEAD_PAIRS, far, 0)


def _attention(q, k, v, *, nb, tb):
    nhp, t, _ = q.shape
    l = t // nb
    nq = l // tb
    tri = (lax.broadcasted_iota(jnp.int32, (tb, tb), 0) >= lax.broadcasted_iota(jnp.int32, (tb, tb), 1)).astype(BF16)
    qspec = pl.BlockSpec((nhp, tb, LANES), lambda b, i: (0, b * nq + i, 0))
    kvspec = pl.BlockSpec((nhp, l, LANES), lambda b, i: (0, b, 0))
    return pl.pallas_call(
        functools.partial(_attn_kernel, tb=tb),
        grid=(nb, nq),
        in_specs=[qspec, kvspec, kvspec, pl.BlockSpec((tb, tb), lambda b, i: (0, 0))],
        out_specs=qspec,
        out_shape=jax.ShapeDtypeStruct(q.shape, BF16),
        scratch_shapes=[pltpu.VMEM((nhp, tb, LANES), F32), pltpu.VMEM((nhp, 2 * tb, 1), F32)],
        compiler_params=pltpu.CompilerParams(
            dimension_semantics=("arbitrary", "arbitrary"), vmem_limit_bytes=VMEM_LIMIT),
        name="attention",
    )(q, k, v, tri)


def _out_kernel(x_ref, yg_ref, zs_ref, ya_ref, za_ref, gs_ref, ga_ref,
                wglu_ref, bglu_ref, wbs_ref, wba_ref, wout_ref, g_ref, o_ref):
    gl = jnp.dot(yg_ref[...], wglu_ref[...].astype(BF16), preferred_element_type=F32) + bglu_ref[...]
    ys = gl[:, :SSM_WIDTH] * _sigmoid(gl[:, SSM_WIDTH:]) * zs_ref[...].astype(F32)
    ya = jnp.concatenate([ya_ref[hp] for hp in range(N_HEAD_PAIRS)], axis=1).astype(F32) * za_ref[...].astype(F32)
    merged = (gs_ref[...].astype(F32) * jnp.dot(ys.astype(BF16), wbs_ref[...].astype(BF16), preferred_element_type=F32)
              + ga_ref[...].astype(F32) * jnp.dot(ya.astype(BF16), wba_ref[...].astype(BF16), preferred_element_type=F32))
    out = jnp.dot(merged.astype(BF16), wout_ref[...].astype(BF16), preferred_element_type=F32)
    ms = jnp.mean(out * out, axis=-1, keepdims=True)
    o_ref[...] = x_ref[...] + out * lax.rsqrt(ms + EPS) * g_ref[...]


def _out_block(x, yg, zs, ya, za, gs, ga, wglu, bglu, wbs, wba, wout, g, li, *, tm):
    t = x.shape[0]
    npb = yg.shape[1] // tm
    row = lambda n: pl.BlockSpec((tm, n), lambda i: (i, 0))
    seq = pl.BlockSpec((None, tm, SSM_WIDTH), lambda i: (i // npb, i % npb, 0))
    full = lambda a: _layer_spec(a, li)
    weights = (wglu, bglu, wbs, wba, wout, g)
    return pl.pallas_call(
        _out_kernel,
        grid=(t // tm,),
        in_specs=[row(D_MODEL), seq, row(512),
                  pl.BlockSpec((N_HEAD_PAIRS, tm, LANES), lambda i: (0, i, 0)),
                  row(512), row(D_MODEL), row(D_MODEL)] + [full(a) for a in weights],
        out_specs=row(D_MODEL),
        out_shape=jax.ShapeDtypeStruct(x.shape, F32),
        compiler_params=pltpu.CompilerParams(
            dimension_semantics=("arbitrary",), vmem_limit_bytes=VMEM_LIMIT),
        name="out_block",
    )(x, yg, zs, ya, za, gs, ga, *weights)


def _tile(n, target):
    t = min(n, target)
    assert n % t == 0, (n, target)
    return t


def kernel(x, pre_norm_g, post_norm_g, w_in, ssm_a_re, ssm_a_im, ssm_log_dt, ssm_b_re, ssm_b_im,
           ssm_c_re, ssm_c_im, ssm_d, w_glu, b_glu, w_branch_ssm, w_branch_attn, w_out):
    nb, l, d = x.shape
    depth = w_in.shape[0]
    assert d == D_MODEL and w_in.shape[2] == IN_COLS
    t = nb * l
    tm = _tile(l, 1024)
    lc = _tile(l, 512)
    tb = _tile(l, 256)

    abar_re, abar_im, wb, wc = _ssm_params(
        ssm_a_re, ssm_a_im, ssm_log_dt[..., None],
        ssm_b_re.transpose(0, 1, 3, 2), ssm_b_im.transpose(0, 1, 3, 2),
        ssm_c_re, ssm_c_im)
    ar = abar_re.reshape(depth, 2, SUBLANES, LANES)
    ai = abar_im.reshape(depth, 2, SUBLANES, LANES)

    row = lambda a: a[:, None, :]
    pre_g, post_g, d_skip, b_glu_r = row(pre_norm_g), row(post_norm_g), row(ssm_d), row(b_glu)

    xt = x.reshape(t, d)
    for li in range(depth):
        u, zs, q, k, v, za, gs, ga = _in_proj(xt, pre_g, w_in, li, nb=nb, tm=tm)
        yg = _ssm(u, wb, ar, ai, wc, d_skip, li, lc=lc)
        ya = _attention(q, k, v, nb=nb, tb=tb)
        xt = _out_block(xt, yg, zs, ya, za, gs, ga,
                        w_glu, b_glu_r, w_branch_ssm, w_branch_attn, w_out, post_g, li, tm=tm)
    return xt.reshape(nb, l, d)
```

```python
import functools
import math

import jax
import jax.numpy as jnp
from jax import lax
from jax.experimental import pallas as pl
from jax.experimental.pallas import tpu as pltpu

F32 = jnp.float32
BF16 = jnp.bfloat16

D_MODEL = 1024
SSM_WIDTH = 512
SSM_GROUP = 16
SSM_GROUPS = 32
SSM_STATE = 64
ATTN_WIDTH = 512
HEAD_DIM = 64
EPS = 1e-6
IN_COLS = 5120

LANES = 128
SUBLANES = 8
N_SLABS = SSM_WIDTH // LANES
SLAB_STATE = (LANES // SSM_GROUP) * SSM_STATE
STATE_CHUNKS = SLAB_STATE // LANES
N_STATE_VREGS = 4
SCAN_UNROLL = 16
N_HEAD_PAIRS = ATTN_WIDTH // LANES

VMEM_LIMIT = 56 * 1024 * 1024

LOG2E = math.log2(math.e)
Q_SCALE = HEAD_DIM ** -0.5 * LOG2E
EXIT_LOG2 = 150.0
NO_BLOCK = 1e30


def _sigmoid(x):
    return 1.0 / (1.0 + jnp.exp(-x))


def _silu(x):
    return x * _sigmoid(x)


def _gelu_tanh(x):
    return 0.5 * x * (1.0 + jnp.tanh(math.sqrt(2.0 / math.pi) * (x + 0.044715 * (x * x * x))))


def _in_proj_kernel(x_ref, g_ref, w_ref, u_ref, zs_ref, q_ref, k_ref, v_ref, za_ref, gs_ref, ga_ref):
    x = x_ref[...]
    ms = jnp.mean(x * x, axis=-1, keepdims=True)
    h = (x * lax.rsqrt(ms + EPS) * g_ref[...]).astype(BF16)

    def proj(lo, hi):
        return jnp.dot(h, w_ref[:, lo:hi].astype(BF16), preferred_element_type=F32)

    gs_ref[...] = _sigmoid(proj(3072, 4096)).astype(BF16)
    ga_ref[...] = _sigmoid(proj(4096, 5120)).astype(BF16)
    zs_ref[...] = _silu(proj(512, 1024)).astype(BF16)
    za_ref[...] = _silu(proj(2560, 3072)).astype(BF16)
    for ref, lo, scale in ((q_ref, 1024, Q_SCALE), (k_ref, 1536, None), (v_ref, 2048, None)):
        y = proj(lo, lo + ATTN_WIDTH)
        if scale is not None:
            y = y * scale
        for hp in range(N_HEAD_PAIRS):
            ref[hp] = y[:, hp * LANES:(hp + 1) * LANES].astype(BF16)
    u_ref[...] = proj(0, 512).astype(BF16)


def _layer_spec(a, li):
    return pl.BlockSpec((None,) + a.shape[1:], lambda *_: (li,) + (0,) * (a.ndim - 1),
                        pipeline_mode=pl.Buffered(1))


def _in_proj(x, g, w, li, *, nb, tm):
    t = x.shape[0]
    npb = t // nb // tm
    row = lambda n: pl.BlockSpec((tm, n), lambda i: (i, 0))
    seq = pl.BlockSpec((None, tm, SSM_WIDTH), lambda i: (i // npb, i % npb, 0))
    full = lambda a: _layer_spec(a, li)
    heads = pl.BlockSpec((N_HEAD_PAIRS, tm, LANES), lambda i: (0, i, 0))
    heads_shape = jax.ShapeDtypeStruct((N_HEAD_PAIRS, t, LANES), BF16)
    flat = lambda n: jax.ShapeDtypeStruct((t, n), BF16)
    return pl.pallas_call(
        _in_proj_kernel,
        grid=(t // tm,),
        in_specs=[row(D_MODEL), full(g), full(w)],
        out_specs=[seq, row(512), heads, heads, heads, row(512), row(1024), row(1024)],
        out_shape=[jax.ShapeDtypeStruct((nb, t // nb, SSM_WIDTH), BF16), flat(512),
                   heads_shape, heads_shape, heads_shape, flat(512), flat(1024), flat(1024)],
        compiler_params=pltpu.CompilerParams(
            dimension_semantics=("arbitrary",), vmem_limit_bytes=VMEM_LIMIT),
        name="in_proj",
    )(x, g, w)


def _ssm_param_kernel(are_ref, aim_ref, ldt_ref, bre_ref, bim_ref, cre_ref, cim_ref,
                      abr_ref, abi_ref, wb_ref, wc_ref):
    a_re = are_ref[0]
    a_im = aim_ref[0]
    dt = jnp.exp(ldt_ref[0])
    mag = jnp.exp(a_re * dt)
    abar_re = mag * jnp.cos(a_im * dt)
    abar_im = mag * jnp.sin(a_im * dt)
    nr = abar_re - 1.0
    ni = abar_im
    den = a_re * a_re + a_im * a_im
    f_re = ((nr * a_re + ni * a_im) / den)[:, None, :]
    f_im = ((ni * a_re - nr * a_im) / den)[:, None, :]
    abr_ref[0] = abar_re
    abi_ref[0] = abar_im
    b_re = bre_ref[0]
    b_im = bim_ref[0]
    bb_re = f_re * b_re - f_im * b_im
    bb_im = f_re * b_im + f_im * b_re

    gps = LANES // SSM_GROUP
    iota = lambda shape, dim: lax.broadcasted_iota(jnp.int32, shape, dim)
    spread = (iota((SSM_STATE, SLAB_STATE), 1) % SSM_STATE == iota((SSM_STATE, SLAB_STATE), 0)).astype(BF16)
    stack = (iota((SLAB_STATE, SSM_STATE), 0) % SSM_STATE == iota((SLAB_STATE, SSM_STATE), 1)).astype(BF16)
    keep_b = iota((LANES, SLAB_STATE), 0) // SSM_GROUP == iota((LANES, SLAB_STATE), 1) // SSM_STATE
    keep_c = iota((SLAB_STATE, LANES), 0) // SSM_STATE == iota((SLAB_STATE, LANES), 1) // SSM_GROUP

    def slab_rows(w, s):
        return w[s * gps:(s + 1) * gps].reshape(LANES, SSM_STATE).astype(BF16)

    for s in range(N_SLABS):
        for part, w in enumerate((bb_re, bb_im)):
            wide = jnp.dot(slab_rows(w, s), spread, preferred_element_type=F32)
            wb_ref[0, s, :, part * SLAB_STATE:(part + 1) * SLAB_STATE] = jnp.where(keep_b, wide, 0.0).astype(BF16)
        for part, (w, sign) in enumerate(((cre_ref[0], 1.0), (cim_ref[0], -1.0))):
            tall = lax.dot_general(stack, slab_rows(w, s), (((1,), (1,)), ((), ())),
                                   preferred_element_type=F32)
            wc_ref[0, s, part * SLAB_STATE:(part + 1) * SLAB_STATE, :] = jnp.where(keep_c, sign * tall, 0.0).astype(BF16)


def _ssm_params(a_re, a_im, log_dt, b_re_t, b_im_t, c_re, c_im):
    depth = a_re.shape[0]
    spec = lambda shape: pl.BlockSpec((1,) + shape[1:], lambda l: (l,) + (0,) * (len(shape) - 1))
    ins = (a_re, a_im, log_dt, b_re_t, b_im_t, c_re, c_im)
    out_shapes = [jax.ShapeDtypeStruct(a_re.shape, F32), jax.ShapeDtypeStruct(a_re.shape, F32),
                  jax.ShapeDtypeStruct((depth, N_SLABS, LANES, 2 * SLAB_STATE), BF16),
                  jax.ShapeDtypeStruct((depth, N_SLABS, 2 * SLAB_STATE, LANES), BF16)]
    return pl.pallas_call(
        _ssm_param_kernel,
        grid=(depth,),
        in_specs=[spec(a.shape) for a in ins],
        out_specs=[spec(o.shape) for o in out_shapes],
        out_shape=out_shapes,
        name="ssm_params",
    )(*ins)


def _ssm_kernel(u_ref, wb_ref, ar_ref, ai_ref, wc_ref, d_ref, y_ref, bu_scr, x_scr, st_scr, *, nb, lc):
    @pl.when(pl.program_id(0) == 0)
    def _():
        st_scr[...] = jnp.zeros_like(st_scr)

    nt = lc // SUBLANES

    for b in range(nb):
        for s in range(N_SLABS):
            bu = jnp.dot(u_ref[b, :, s * LANES:(s + 1) * LANES], wb_ref[s], preferred_element_type=F32)
            vp, base = s // 2, (s % 2) * STATE_CHUNKS
            for c in range(STATE_CHUNKS):
                j = base + c
                re = bu[:, c * LANES:(c + 1) * LANES]
                im = bu[:, SLAB_STATE + c * LANES:SLAB_STATE + (c + 1) * LANES]
                bu_scr[b, 2 * vp, :, j * SUBLANES:(j + 1) * SUBLANES, :] = re.reshape(nt, SUBLANES, LANES)
                bu_scr[b, 2 * vp + 1, :, j * SUBLANES:(j + 1) * SUBLANES, :] = im.reshape(nt, SUBLANES, LANES)

    a_r = [ar_ref[0], ar_ref[1]]
    a_i = [ai_ref[0], ai_ref[1]]

    def steps(it, carry):
        xs = list(carry)
        for t in range(SCAN_UNROLL):
            i, r = it * (SCAN_UNROLL // SUBLANES) + t // SUBLANES, t % SUBLANES
            row = pl.multiple_of(i * (SUBLANES * SUBLANES) + r * SUBLANES, SUBLANES)
            for b in range(nb):
                for vp in range(2):
                    k = (b * 2 + vp) * 2
                    b_r = bu_scr[b, 2 * vp, i, pl.ds(r, SUBLANES, stride=SUBLANES), :]
                    b_i = bu_scr[b, 2 * vp + 1, i, pl.ds(r, SUBLANES, stride=SUBLANES), :]
                    x_r, x_i = xs[k], xs[k + 1]
                    n_r = a_r[vp] * x_r - a_i[vp] * x_i + b_r
                    n_i = a_r[vp] * x_i + a_i[vp] * x_r + b_i
                    x_scr[b, 2 * vp, pl.ds(row, SUBLANES), :] = n_r
                    x_scr[b, 2 * vp + 1, pl.ds(row, SUBLANES), :] = n_i
                    xs[k], xs[k + 1] = n_r, n_i
        return tuple(xs)

    init = tuple(st_scr[b, v] for b in range(nb) for v in range(N_STATE_VREGS))
    final = lax.fori_loop(0, lc // SCAN_UNROLL, steps, init)
    for b in range(nb):
        for v in range(N_STATE_VREGS):
            st_scr[b, v] = final[b * N_STATE_VREGS + v]

    for b in range(nb):
        for s in range(N_SLABS):
            vp, base = s // 2, (s % 2) * STATE_CHUNKS
            parts = [x_scr[b, 2 * vp + ri, pl.ds(base + c, lc, stride=SUBLANES), :]
                     for ri in range(2) for c in range(STATE_CHUNKS)]
            xs = jnp.concatenate(parts, axis=1).astype(BF16)
            y = jnp.dot(xs, wc_ref[s], preferred_element_type=F32)
            sl = slice(s * LANES, (s + 1) * LANES)
            y = y + d_ref[:, sl] * u_ref[b, :, sl].astype(F32)
            y_ref[b, :, sl] = _gelu_tanh(y).astype(BF16)


def _ssm(u, wb, ar, ai, wc, d, li, *, lc):
    nb, l, _ = u.shape
    full = lambda a: _layer_spec(a, li)
    blk = pl.BlockSpec((nb, lc, SSM_WIDTH), lambda c: (0, c, 0))
    return pl.pallas_call(
        functools.partial(_ssm_kernel, nb=nb, lc=lc),
        grid=(l // lc,),
        in_specs=[blk, full(wb), full(ar), full(ai), full(wc), full(d)],
        out_specs=blk,
        out_shape=jax.ShapeDtypeStruct(u.shape, BF16),
        scratch_shapes=[
            pltpu.VMEM((nb, N_STATE_VREGS, lc // SUBLANES, SUBLANES * SUBLANES, LANES), F32),
            pltpu.VMEM((nb, N_STATE_VREGS, lc * SUBLANES, LANES), F32),
            pltpu.VMEM((nb, N_STATE_VREGS, SUBLANES, LANES), F32),
        ],
        compiler_params=pltpu.CompilerParams(
            dimension_semantics=("arbitrary",), vmem_limit_bytes=VMEM_LIMIT),
        name="ssm",
    )(u, wb, ar, ai, wc, d)


def _softplus2(z):
    return jnp.maximum(z, 0.0) + jnp.log2(1.0 + jnp.exp2(-jnp.abs(z)))


def _attn_kernel(q_ref, k_ref, v_ref, tri_ref, o_ref, acc_scr, carry_scr, *, tb):
    qi = pl.program_id(1)
    low = lax.broadcasted_iota(jnp.int32, (tb, LANES), 1) < HEAD_DIM
    causal = (lax.broadcasted_iota(jnp.int32, (tb, tb), 1)
              < lax.broadcasted_iota(jnp.int32, (tb, tb), 0))
    tri = tri_ref[...]
    start0 = pl.multiple_of(qi * tb, tb)
    start1 = pl.multiple_of(jnp.maximum(qi - 1, 0) * tb, tb)
    no_prev = jnp.where(qi > 0, 0.0, NO_BLOCK)

    causal2 = jnp.concatenate([causal, causal], axis=0)
    half = tb // 2
    first = tb // 4

    def stacked_q(hp):
        q = q_ref[hp]
        zero = jnp.zeros_like(q)
        return jnp.concatenate([jnp.where(low, q, zero), jnp.where(low, zero, q)], axis=0)

    def scores(hp, q2, start):
        return lax.dot_general(q2, k_ref[hp, pl.ds(start, tb), :], (((1,), (1,)), ((), ())),
                               preferred_element_type=F32)

    def weighted_v(hp, w, start):
        return jnp.dot(w.astype(BF16), v_ref[hp, pl.ds(start, tb), :], preferred_element_type=F32)

    def unstack(x):
        return jnp.where(low, x[:tb], x[tb:])

    def near_prev(f, *tiles, dtype):
        rows = []
        for h in range(2):
            top = slice(h * tb, h * tb + first)
            bot = slice(h * tb + first, (h + 1) * tb)
            rows.append(f(*(t[top, :] for t in tiles)).astype(dtype))
            rows.append(jnp.concatenate([jnp.zeros((tb - first, half), dtype),
                                         f(*(t[bot, half:] for t in tiles)).astype(dtype)], axis=1))
        return jnp.concatenate(rows, axis=0)

    def stage_a(hp):
        q2 = stacked_q(hp)
        z0 = scores(hp, q2, start0)
        z1 = scores(hp, q2, start1)
        sp = jnp.concatenate([jnp.where(causal2, _softplus2(z0), 0.0).astype(BF16),
                              near_prev(_softplus2, z1, dtype=BF16)], axis=0)
        return z0, z1, sp

    def stage_b(hp, z0, z1, sp):
        cum = jnp.dot(sp, tri, preferred_element_type=F32)
        cum0, cum1 = cum[:2 * tb], cum[2 * tb:]
        c0 = cum0[:, 0:1]
        w0 = jnp.where(causal2, jnp.exp2(z0 - cum0), 0.0).astype(BF16)
        c0p = jnp.broadcast_to(c0 + no_prev, (2 * tb, tb))
        w1 = near_prev(lambda z, c, p: jnp.exp2(z - c - p), z1, cum1, c0p, dtype=BF16)
        return w0, w1, c0 + cum1[:, 0:1]

    def stage_c(hp, w0, w1, carry, min_carry):
        out = unstack(jnp.dot(w0, v_ref[hp, pl.ds(start0, tb), :], preferred_element_type=F32)
                      + jnp.dot(w1, v_ref[hp, pl.ds(start1, tb), :], preferred_element_type=F32))
        carry_scr[hp] = carry
        acc_scr[hp] = out
        o_ref[hp] = out.astype(o_ref.dtype)
        return jnp.minimum(min_carry, jnp.minimum(carry[:tb], carry[tb:]))

    min_carry = jnp.full((tb, 1), jnp.inf, F32)
    a_out, b_out = {}, {}
    for step in range(N_HEAD_PAIRS + 2):
        if step < N_HEAD_PAIRS:
            a_out[step] = stage_a(step)
        if 0 <= step - 1 < N_HEAD_PAIRS:
            b_out[step - 1] = stage_b(step - 1, *a_out.pop(step - 1))
        if 0 <= step - 2 < N_HEAD_PAIRS:
            min_carry = stage_c(step - 2, *b_out.pop(step - 2), min_carry)

    @pl.when(jnp.logical_and(qi >= 1, jnp.min(min_carry) < EXIT_LOG2))
    def _():
        row_in_block = lax.broadcasted_iota(jnp.int32, (2 * tb, tb), 0) % tb
        skipped = jnp.logical_and(row_in_block >= first,
                                  lax.broadcasted_iota(jnp.int32, (2 * tb, tb), 1) < half)

        def far(hp, _):
            q2 = stacked_q(hp)

            z = scores(hp, q2, start1)
            sp = jnp.where(skipped, _softplus2(z), 0.0)
            cum = jnp.dot(sp.astype(BF16), tri, preferred_element_type=F32)
            carry = carry_scr[hp]
            w = jnp.where(skipped, jnp.exp2(z - cum - carry), 0.0)
            acc0 = weighted_v(hp, w, start1)
            carry0 = carry + cum[:, 0:1]

            def cond(st):
                j, carry, _ = st
                return jnp.logical_and(j >= 0, jnp.min(carry) < EXIT_LOG2)

            def body(st):
                j, carry, acc = st
                start = pl.multiple_of(j * tb, tb)
                z = scores(hp, q2, start)
                cum = jnp.dot(_softplus2(z).astype(BF16), tri, preferred_element_type=F32)
                w = jnp.exp2(z - cum - carry)
                return j - 1, carry + cum[:, 0:1], acc + weighted_v(hp, w, start)

            init = (qi - 2, carry0, acc0)
            acc = lax.while_loop(cond, body, init)[2]
            o_ref[hp] = (acc_scr[hp] + unstack(acc)).astype(o_ref.dtype)
            return 0

        lax.fori_loop(0, N_HEAD_PAIRS, far, 0)


def _attention(q, k, v, *, nb, tb):
    nhp, t, _ = q.shape
    l = t // nb
    nq = l // tb
    tri = (lax.broadcasted_iota(jnp.int32, (tb, tb), 0) >= lax.broadcasted_iota(jnp.int32, (tb, tb), 1)).astype(BF16)
    qspec = pl.BlockSpec((nhp, tb, LANES), lambda b, i: (0, b * nq + i, 0))
    kvspec = pl.BlockSpec((nhp, l, LANES), lambda b, i: (0, b, 0))
    return pl.pallas_call(
        functools.partial(_attn_kernel, tb=tb),
        grid=(nb, nq),
        in_specs=[qspec, kvspec, kvspec, pl.BlockSpec((tb, tb), lambda b, i: (0, 0))],
        out_specs=qspec,
        out_shape=jax.ShapeDtypeStruct(q.shape, BF16),
        scratch_shapes=[pltpu.VMEM((nhp, tb, LANES), F32), pltpu.VMEM((nhp, 2 * tb, 1), F32)],
        compiler_params=pltpu.CompilerParams(
            dimension_semantics=("arbitrary", "arbitrary"), vmem_limit_bytes=VMEM_LIMIT),
        name="attention",
    )(q, k, v, tri)


def _out_kernel(x_ref, yg_ref, zs_ref, ya_ref, za_ref, gs_ref, ga_ref,
                wglu_ref, bglu_ref, wbs_ref, wba_ref, wout_ref, g_ref, o_ref):
    gl = jnp.dot(yg_ref[...], wglu_ref[...].astype(BF16), preferred_element_type=F32) + bglu_ref[...]
    ys = gl[:, :SSM_WIDTH] * _sigmoid(gl[:, SSM_WIDTH:]) * zs_ref[...].astype(F32)
    ya = jnp.concatenate([ya_ref[hp] for hp in range(N_HEAD_PAIRS)], axis=1).astype(F32) * za_ref[...].astype(F32)
    merged = (gs_ref[...].astype(F32) * jnp.dot(ys.astype(BF16), wbs_ref[...].astype(BF16), preferred_element_type=F32)
              + ga_ref[...].astype(F32) * jnp.dot(ya.astype(BF16), wba_ref[...].astype(BF16), preferred_element_type=F32))
    out = jnp.dot(merged.astype(BF16), wout_ref[...].astype(BF16), preferred_element_type=F32)
    ms = jnp.mean(out * out, axis=-1, keepdims=True)
    o_ref[...] = x_ref[...] + out * lax.rsqrt(ms + EPS) * g_ref[...]


def _out_block(x, yg, zs, ya, za, gs, ga, wglu, bglu, wbs, wba, wout, g, li, *, tm):
    t = x.shape[0]
    npb = yg.shape[1] // tm
    row = lambda n: pl.BlockSpec((tm, n), lambda i: (i, 0))
    seq = pl.BlockSpec((None, tm, SSM_WIDTH), lambda i: (i // npb, i % npb, 0))
    full = lambda a: _layer_spec(a, li)
    weights = (wglu, bglu, wbs, wba, wout, g)
    return pl.pallas_call(
        _out_kernel,
        grid=(t // tm,),
        in_specs=[row(D_MODEL), seq, row(512),
                  pl.BlockSpec((N_HEAD_PAIRS, tm, LANES), lambda i: (0, i, 0)),
                  row(512), row(D_MODEL), row(D_MODEL)] + [full(a) for a in weights],
        out_specs=row(D_MODEL),
        out_shape=jax.ShapeDtypeStruct(x.shape, F32),
        compiler_params=pltpu.CompilerParams(
            dimension_semantics=("arbitrary",), vmem_limit_bytes=VMEM_LIMIT),
        name="out_block",
    )(x, yg, zs, ya, za, gs, ga, *weights)


def _tile(n, target):
    t = min(n, target)
    assert n % t == 0, (n, target)
    return t


def kernel(x, pre_norm_g, post_norm_g, w_in, ssm_a_re, ssm_a_im, ssm_log_dt, ssm_b_re, ssm_b_im,
           ssm_c_re, ssm_c_im, ssm_d, w_glu, b_glu, w_branch_ssm, w_branch_attn, w_out):
    nb, l, d = x.shape
    depth = w_in.shape[0]
    assert d == D_MODEL and w_in.shape[2] == IN_COLS
    t = nb * l
    tm = _tile(l, 1024)
    lc = _tile(l, 512)
    tb = _tile(l, 256)

    abar_re, abar_im, wb, wc = _ssm_params(
        ssm_a_re, ssm_a_im, ssm_log_dt[..., None],
        ssm_b_re.transpose(0, 1, 3, 2), ssm_b_im.transpose(0, 1, 3, 2),
        ssm_c_re, ssm_c_im)
    ar = abar_re.reshape(depth, 2, SUBLANES, LANES)
    ai = abar_im.reshape(depth, 2, SUBLANES, LANES)

    row = lambda a: a[:, None, :]
    pre_g, post_g, d_skip, b_glu_r = row(pre_norm_g), row(post_norm_g), row(ssm_d), row(b_glu)

    xt = x.reshape(t, d)
    for li in range(depth):
        u, zs, q, k, v, za, gs, ga = _in_proj(xt, pre_g, w_in, li, nb=nb, tm=tm)
        yg = _ssm(u, wb, ar, ai, wc, d_skip, li, lc=lc)
        ya = _attention(q, k, v, nb=nb, tb=tb)
        xt = _out_block(xt, yg, zs, ya, za, gs, ga,
                        w_glu, b_glu_r, w_branch_ssm, w_branch_attn, w_out, post_g, li, tm=tm)
    return xt.reshape(nb, l, d)
```

```python
import functools
import math

import jax
import jax.numpy as jnp
from jax import lax
from jax.experimental import pallas as pl
from jax.experimental.pallas import tpu as pltpu

F32 = jnp.float32
BF16 = jnp.bfloat16

D_MODEL = 1024
SSM_WIDTH = 512
SSM_GROUP = 16
SSM_GROUPS = 32
SSM_STATE = 64
ATTN_WIDTH = 512
HEAD_DIM = 64
EPS = 1e-6
IN_COLS = 5120

LANES = 128
SUBLANES = 8
N_SLABS = SSM_WIDTH // LANES
SLAB_STATE = (LANES // SSM_GROUP) * SSM_STATE
STATE_CHUNKS = SLAB_STATE // LANES
N_STATE_VREGS = 4
SCAN_UNROLL = 32
N_HEAD_PAIRS = ATTN_WIDTH // LANES

VMEM_LIMIT = 56 * 1024 * 1024

LOG2E = math.log2(math.e)
Q_SCALE = HEAD_DIM ** -0.5 * LOG2E
EXIT_LOG2 = 150.0
NO_BLOCK = 1e30


def _sigmoid(x):
    return 1.0 / (1.0 + jnp.exp(-x))


def _silu(x):
    return x * _sigmoid(x)


def _gelu_tanh(x):
    return 0.5 * x * (1.0 + jnp.tanh(math.sqrt(2.0 / math.pi) * (x + 0.044715 * (x * x * x))))


def _in_proj_kernel(x_ref, g_ref, w_ref, u_ref, zs_ref, q_ref, k_ref, v_ref, za_ref, gs_ref, ga_ref):
    x = x_ref[...]
    ms = jnp.mean(x * x, axis=-1, keepdims=True)
    h = (x * lax.rsqrt(ms + EPS) * g_ref[...]).astype(BF16)

    def proj(lo, hi):
        return jnp.dot(h, w_ref[:, lo:hi].astype(BF16), preferred_element_type=F32)

    gs_ref[...] = _sigmoid(proj(3072, 4096)).astype(BF16)
    ga_ref[...] = _sigmoid(proj(4096, 5120)).astype(BF16)
    zs_ref[...] = _silu(proj(512, 1024)).astype(BF16)
    za_ref[...] = _silu(proj(2560, 3072)).astype(BF16)
    for ref, lo, scale in ((q_ref, 1024, Q_SCALE), (k_ref, 1536, None), (v_ref, 2048, None)):
        y = proj(lo, lo + ATTN_WIDTH)
        if scale is not None:
            y = y * scale
        for hp in range(N_HEAD_PAIRS):
            ref[hp] = y[:, hp * LANES:(hp + 1) * LANES].astype(BF16)
    u_ref[...] = proj(0, 512).astype(BF16)


def _layer_spec(a, li):
    return pl.BlockSpec((None,) + a.shape[1:], lambda *_: (li,) + (0,) * (a.ndim - 1),
                        pipeline_mode=pl.Buffered(1))


def _in_proj(x, g, w, li, *, nb, tm):
    t = x.shape[0]
    npb = t // nb // tm
    row = lambda n: pl.BlockSpec((tm, n), lambda i: (i, 0))
    seq = pl.BlockSpec((None, tm, SSM_WIDTH), lambda i: (i // npb, i % npb, 0))
    full = lambda a: _layer_spec(a, li)
    heads = pl.BlockSpec((N_HEAD_PAIRS, tm, LANES), lambda i: (0, i, 0))
    heads_shape = jax.ShapeDtypeStruct((N_HEAD_PAIRS, t, LANES), BF16)
    flat = lambda n: jax.ShapeDtypeStruct((t, n), BF16)
    return pl.pallas_call(
        _in_proj_kernel,
        grid=(t // tm,),
        in_specs=[row(D_MODEL), full(g), full(w)],
        out_specs=[seq, row(512), heads, heads, heads, row(512), row(1024), row(1024)],
        out_shape=[jax.ShapeDtypeStruct((nb, t // nb, SSM_WIDTH), BF16), flat(512),
                   heads_shape, heads_shape, heads_shape, flat(512), flat(1024), flat(1024)],
        compiler_params=pltpu.CompilerParams(
            dimension_semantics=("arbitrary",), vmem_limit_bytes=VMEM_LIMIT),
        name="in_proj",
    )(x, g, w)


def _ssm_param_kernel(are_ref, aim_ref, ldt_ref, bre_ref, bim_ref, cre_ref, cim_ref,
                      abr_ref, abi_ref, wb_ref, wc_ref):
    a_re = are_ref[0]
    a_im = aim_ref[0]
    dt = jnp.exp(ldt_ref[0])
    mag = jnp.exp(a_re * dt)
    abar_re = mag * jnp.cos(a_im * dt)
    abar_im = mag * jnp.sin(a_im * dt)
    nr = abar_re - 1.0
    ni = abar_im
    den = a_re * a_re + a_im * a_im
    f_re = ((nr * a_re + ni * a_im) / den)[:, None, :]
    f_im = ((ni * a_re - nr * a_im) / den)[:, None, :]
    abr_ref[0] = abar_re
    abi_ref[0] = abar_im
    b_re = bre_ref[0]
    b_im = bim_ref[0]
    bb_re = f_re * b_re - f_im * b_im
    bb_im = f_re * b_im + f_im * b_re

    gps = LANES // SSM_GROUP
    iota = lambda shape, dim: lax.broadcasted_iota(jnp.int32, shape, dim)
    spread = (iota((SSM_STATE, SLAB_STATE), 1) % SSM_STATE == iota((SSM_STATE, SLAB_STATE), 0)).astype(BF16)
    stack = (iota((SLAB_STATE, SSM_STATE), 0) % SSM_STATE == iota((SLAB_STATE, SSM_STATE), 1)).astype(BF16)
    keep_b = iota((LANES, SLAB_STATE), 0) // SSM_GROUP == iota((LANES, SLAB_STATE), 1) // SSM_STATE
    keep_c = iota((SLAB_STATE, LANES), 0) // SSM_STATE == iota((SLAB_STATE, LANES), 1) // SSM_GROUP

    def slab_rows(w, s):
        return w[s * gps:(s + 1) * gps].reshape(LANES, SSM_STATE).astype(BF16)

    for s in range(N_SLABS):
        for part, w in enumerate((bb_re, bb_im)):
            wide = jnp.dot(slab_rows(w, s), spread, preferred_element_type=F32)
            wb_ref[0, s, :, part * SLAB_STATE:(part + 1) * SLAB_STATE] = jnp.where(keep_b, wide, 0.0).astype(BF16)
        for part, (w, sign) in enumerate(((cre_ref[0], 1.0), (cim_ref[0], -1.0))):
            tall = lax.dot_general(stack, slab_rows(w, s), (((1,), (1,)), ((), ())),
                                   preferred_element_type=F32)
            wc_ref[0, s, part * SLAB_STATE:(part + 1) * SLAB_STATE, :] = jnp.where(keep_c, sign * tall, 0.0).astype(BF16)


def _ssm_params(a_re, a_im, log_dt, b_re_t, b_im_t, c_re, c_im):
    depth = a_re.shape[0]
    spec = lambda shape: pl.BlockSpec((1,) + shape[1:], lambda l: (l,) + (0,) * (len(shape) - 1))
    ins = (a_re, a_im, log_dt, b_re_t, b_im_t, c_re, c_im)
    out_shapes = [jax.ShapeDtypeStruct(a_re.shape, F32), jax.ShapeDtypeStruct(a_re.shape, F32),
                  jax.ShapeDtypeStruct((depth, N_SLABS, LANES, 2 * SLAB_STATE), BF16),
                  jax.ShapeDtypeStruct((depth, N_SLABS, 2 * SLAB_STATE, LANES), BF16)]
    return pl.pallas_call(
        _ssm_param_kernel,
        grid=(depth,),
        in_specs=[spec(a.shape) for a in ins],
        out_specs=[spec(o.shape) for o in out_shapes],
        out_shape=out_shapes,
        name="ssm_params",
    )(*ins)


def _ssm_kernel(u_ref, wb_ref, ar_ref, ai_ref, wc_ref, d_ref, y_ref, bu_scr, x_scr, st_scr, *, nb, lc):
    @pl.when(pl.program_id(0) == 0)
    def _():
        st_scr[...] = jnp.zeros_like(st_scr)

    nt = lc // SUBLANES

    for b in range(nb):
        for s in range(N_SLABS):
            bu = jnp.dot(u_ref[b, :, s * LANES:(s + 1) * LANES], wb_ref[s], preferred_element_type=F32)
            vp, base = s // 2, (s % 2) * STATE_CHUNKS
            for c in range(STATE_CHUNKS):
                j = base + c
                re = bu[:, c * LANES:(c + 1) * LANES]
                im = bu[:, SLAB_STATE + c * LANES:SLAB_STATE + (c + 1) * LANES]
                bu_scr[b, 2 * vp, :, j * SUBLANES:(j + 1) * SUBLANES, :] = re.reshape(nt, SUBLANES, LANES)
                bu_scr[b, 2 * vp + 1, :, j * SUBLANES:(j + 1) * SUBLANES, :] = im.reshape(nt, SUBLANES, LANES)

    a_r = [ar_ref[0], ar_ref[1]]
    a_i = [ai_ref[0], ai_ref[1]]

    def steps(it, carry):
        xs = list(carry)
        for t in range(SCAN_UNROLL):
            i, r = it * (SCAN_UNROLL // SUBLANES) + t // SUBLANES, t % SUBLANES
            row = pl.multiple_of(i * (SUBLANES * SUBLANES) + r * SUBLANES, SUBLANES)
            for b in range(nb):
                for vp in range(2):
                    k = (b * 2 + vp) * 2
                    b_r = bu_scr[b, 2 * vp, i, pl.ds(r, SUBLANES, stride=SUBLANES), :]
                    b_i = bu_scr[b, 2 * vp + 1, i, pl.ds(r, SUBLANES, stride=SUBLANES), :]
                    x_r, x_i = xs[k], xs[k + 1]
                    n_r = a_r[vp] * x_r - a_i[vp] * x_i + b_r
                    n_i = a_r[vp] * x_i + a_i[vp] * x_r + b_i
                    x_scr[b, 2 * vp, pl.ds(row, SUBLANES), :] = n_r
                    x_scr[b, 2 * vp + 1, pl.ds(row, SUBLANES), :] = n_i
                    xs[k], xs[k + 1] = n_r, n_i
        return tuple(xs)

    init = tuple(st_scr[b, v] for b in range(nb) for v in range(N_STATE_VREGS))
    final = lax.fori_loop(0, lc // SCAN_UNROLL, steps, init)
    for b in range(nb):
        for v in range(N_STATE_VREGS):
            st_scr[b, v] = final[b * N_STATE_VREGS + v]

    for b in range(nb):
        for s in range(N_SLABS):
            vp, base = s // 2, (s % 2) * STATE_CHUNKS
            parts = [x_scr[b, 2 * vp + ri, pl.ds(base + c, lc, stride=SUBLANES), :]
                     for ri in range(2) for c in range(STATE_CHUNKS)]
            xs = jnp.concatenate(parts, axis=1).astype(BF16)
            y = jnp.dot(xs, wc_ref[s], preferred_element_type=F32)
            sl = slice(s * LANES, (s + 1) * LANES)
            y = y + d_ref[:, sl] * u_ref[b, :, sl].astype(F32)
            y_ref[b, :, sl] = _gelu_tanh(y).astype(BF16)


def _ssm(u, wb, ar, ai, wc, d, li, *, lc):
    nb, l, _ = u.shape
    full = lambda a: _layer_spec(a, li)
    blk = pl.BlockSpec((nb, lc, SSM_WIDTH), lambda c: (0, c, 0))
    return pl.pallas_call(
        functools.partial(_ssm_kernel, nb=nb, lc=lc),
        grid=(l // lc,),
        in_specs=[blk, full(wb), full(ar), full(ai), full(wc), full(d)],
        out_specs=blk,
        out_shape=jax.ShapeDtypeStruct(u.shape, BF16),
        scratch_shapes=[
            pltpu.VMEM((nb, N_STATE_VREGS, lc // SUBLANES, SUBLANES * SUBLANES, LANES), F32),
            pltpu.VMEM((nb, N_STATE_VREGS, lc * SUBLANES, LANES), F32),
            pltpu.VMEM((nb, N_STATE_VREGS, SUBLANES, LANES), F32),
        ],
        compiler_params=pltpu.CompilerParams(
            dimension_semantics=("arbitrary",), vmem_limit_bytes=VMEM_LIMIT),
        name="ssm",
    )(u, wb, ar, ai, wc, d)


def _softplus2(z):
    return jnp.maximum(z, 0.0) + jnp.log2(1.0 + jnp.exp2(-jnp.abs(z)))


def _attn_kernel(q_ref, k_ref, v_ref, tri_ref, o_ref, acc_scr, carry_scr, *, tb):
    qi = pl.program_id(1)
    low = lax.broadcasted_iota(jnp.int32, (tb, LANES), 1) < HEAD_DIM
    causal = (lax.broadcasted_iota(jnp.int32, (tb, tb), 1)
              < lax.broadcasted_iota(jnp.int32, (tb, tb), 0))
    tri = tri_ref[...]
    start0 = pl.multiple_of(qi * tb, tb)
    start1 = pl.multiple_of(jnp.maximum(qi - 1, 0) * tb, tb)
    no_prev = jnp.where(qi > 0, 0.0, NO_BLOCK)

    causal2 = jnp.concatenate([causal, causal], axis=0)
    half = tb // 2
    first = tb // 4

    def stacked_q(hp):
        q = q_ref[hp]
        zero = jnp.zeros_like(q)
        return jnp.concatenate([jnp.where(low, q, zero), jnp.where(low, zero, q)], axis=0)

    def scores(hp, q2, start):
        return lax.dot_general(q2, k_ref[hp, pl.ds(start, tb), :], (((1,), (1,)), ((), ())),
                               preferred_element_type=F32)

    def weighted_v(hp, w, start):
        return jnp.dot(w.astype(BF16), v_ref[hp, pl.ds(start, tb), :], preferred_element_type=F32)

    def unstack(x):
        return jnp.where(low, x[:tb], x[tb:])

    def near_prev(f, *tiles, dtype):
        rows = []
        for h in range(2):
            top = slice(h * tb, h * tb + first)
            bot = slice(h * tb + first, (h + 1) * tb)
            rows.append(f(*(t[top, :] for t in tiles)).astype(dtype))
            rows.append(jnp.concatenate([jnp.zeros((tb - first, half), dtype),
                                         f(*(t[bot, half:] for t in tiles)).astype(dtype)], axis=1))
        return jnp.concatenate(rows, axis=0)

    def stage_a(hp):
        q2 = stacked_q(hp)
        z0 = scores(hp, q2, start0)
        z1 = scores(hp, q2, start1)
        sp = jnp.concatenate([jnp.where(causal2, _softplus2(z0), 0.0).astype(BF16),
                              near_prev(_softplus2, z1, dtype=BF16)], axis=0)
        return z0, z1, sp

    def stage_b(hp, z0, z1, sp):
        cum = jnp.dot(sp, tri, preferred_element_type=F32)
        cum0, cum1 = cum[:2 * tb], cum[2 * tb:]
        c0 = cum0[:, 0:1]
        w0 = jnp.where(causal2, jnp.exp2(z0 - cum0), 0.0).astype(BF16)
        c0p = jnp.broadcast_to(c0 + no_prev, (2 * tb, tb))
        w1 = near_prev(lambda z, c, p: jnp.exp2(z - c - p), z1, cum1, c0p, dtype=BF16)
        return w0, w1, c0 + cum1[:, 0:1]

    def stage_c(hp, w0, w1, carry, min_carry):
        out = unstack(jnp.dot(w0, v_ref[hp, pl.ds(start0, tb), :], preferred_element_type=F32)
                      + jnp.dot(w1, v_ref[hp, pl.ds(start1, tb), :], preferred_element_type=F32))
        carry_scr[hp] = carry
        acc_scr[hp] = out
        o_ref[hp] = out.astype(o_ref.dtype)
        return jnp.minimum(min_carry, jnp.minimum(carry[:tb], carry[tb:]))

    min_carry = jnp.full((tb, 1), jnp.inf, F32)
    a_out, b_out = {}, {}
    for step in range(N_HEAD_PAIRS + 2):
        if step < N_HEAD_PAIRS:
            a_out[step] = stage_a(step)
        if 0 <= step - 1 < N_HEAD_PAIRS:
            b_out[step - 1] = stage_b(step - 1, *a_out.pop(step - 1))
        if 0 <= step - 2 < N_HEAD_PAIRS:
            min_carry = stage_c(step - 2, *b_out.pop(step - 2), min_carry)

    @pl.when(jnp.logical_and(qi >= 1, jnp.min(min_carry) < EXIT_LOG2))
    def _():
        row_in_block = lax.broadcasted_iota(jnp.int32, (2 * tb, tb), 0) % tb
        skipped = jnp.logical_and(row_in_block >= first,
                                  lax.broadcasted_iota(jnp.int32, (2 * tb, tb), 1) < half)

        def far(hp, _):
            q2 = stacked_q(hp)

            z = scores(hp, q2, start1)
            sp = jnp.where(skipped, _softplus2(z), 0.0)
            cum = jnp.dot(sp.astype(BF16), tri, preferred_element_type=F32)
            carry = carry_scr[hp]
            w = jnp.where(skipped, jnp.exp2(z - cum - carry), 0.0)
            acc0 = weighted_v(hp, w, start1)
            carry0 = carry + cum[:, 0:1]

            def cond(st):
                j, carry, _ = st
                return jnp.logical_and(j >= 0, jnp.min(carry) < EXIT_LOG2)

            def body(st):
                j, carry, acc = st
                start = pl.multiple_of(j * tb, tb)
                z = scores(hp, q2, start)
                cum = jnp.dot(_softplus2(z).astype(BF16), tri, preferred_element_type=F32)
                w = jnp.exp2(z - cum - carry)
                return j - 1, carry + cum[:, 0:1], acc + weighted_v(hp, w, start)

            init = (qi - 2, carry0, acc0)
            acc = lax.while_loop(cond, body, init)[2]
            o_ref[hp] = (acc_scr[hp] + unstack(acc)).astype(o_ref.dtype)
            return 0

        lax.fori_loop(0, N_HEAD_PAIRS, far, 0)


def _attention(q, k, v, *, nb, tb):
    nhp, t, _ = q.shape
    l = t // nb
    nq = l // tb
    tri = (lax.broadcasted_iota(jnp.int32, (tb, tb), 0) >= lax.broadcasted_iota(jnp.int32, (tb, tb), 1)).astype(BF16)
    qspec = pl.BlockSpec((nhp, tb, LANES), lambda b, i: (0, b * nq + i, 0))
    kvspec = pl.BlockSpec((nhp, l, LANES), lambda b, i: (0, b, 0))
    return pl.pallas_call(
        functools.partial(_attn_kernel, tb=tb),
        grid=(nb, nq),
        in_specs=[qspec, kvspec, kvspec, pl.BlockSpec((tb, tb), lambda b, i: (0, 0))],
        out_specs=qspec,
        out_shape=jax.ShapeDtypeStruct(q.shape, BF16),
        scratch_shapes=[pltpu.VMEM((nhp, tb, LANES), F32), pltpu.VMEM((nhp, 2 * tb, 1), F32)],
        compiler_params=pltpu.CompilerParams(
            dimension_semantics=("arbitrary", "arbitrary"), vmem_limit_bytes=VMEM_LIMIT),
        name="attention",
    )(q, k, v, tri)


def _out_kernel(x_ref, yg_ref, zs_ref, ya_ref, za_ref, gs_ref, ga_ref,
                wglu_ref, bglu_ref, wbs_ref, wba_ref, wout_ref, g_ref, o_ref):
    gl = jnp.dot(yg_ref[...], wglu_ref[...].astype(BF16), preferred_element_type=F32) + bglu_ref[...]
    ys = gl[:, :SSM_WIDTH] * _sigmoid(gl[:, SSM_WIDTH:]) * zs_ref[...].astype(F32)
    ya = jnp.concatenate([ya_ref[hp] for hp in range(N_HEAD_PAIRS)], axis=1).astype(F32) * za_ref[...].astype(F32)
    merged = (gs_ref[...].astype(F32) * jnp.dot(ys.astype(BF16), wbs_ref[...].astype(BF16), preferred_element_type=F32)
              + ga_ref[...].astype(F32) * jnp.dot(ya.astype(BF16), wba_ref[...].astype(BF16), preferred_element_type=F32))
    out = jnp.dot(merged.astype(BF16), wout_ref[...].astype(BF16), preferred_element_type=F32)
    ms = jnp.mean(out * out, axis=-1, keepdims=True)
    o_ref[...] = x_ref[...] + out * lax.rsqrt(ms + EPS) * g_ref[...]


def _out_block(x, yg, zs, ya, za, gs, ga, wglu, bglu, wbs, wba, wout, g, li, *, tm):
    t = x.shape[0]
    npb = yg.shape[1] // tm
    row = lambda n: pl.BlockSpec((tm, n), lambda i: (i, 0))
    seq = pl.BlockSpec((None, tm, SSM_WIDTH), lambda i: (i // npb, i % npb, 0))
    full = lambda a: _layer_spec(a, li)
    weights = (wglu, bglu, wbs, wba, wout, g)
    return pl.pallas_call(
        _out_kernel,
        grid=(t // tm,),
        in_specs=[row(D_MODEL), seq, row(512),
                  pl.BlockSpec((N_HEAD_PAIRS, tm, LANES), lambda i: (0, i, 0)),
                  row(512), row(D_MODEL), row(D_MODEL)] + [full(a) for a in weights],
        out_specs=row(D_MODEL),
        out_shape=jax.ShapeDtypeStruct(x.shape, F32),
        compiler_params=pltpu.CompilerParams(
            dimension_semantics=("arbitrary",), vmem_limit_bytes=VMEM_LIMIT),
        name="out_block",
    )(x, yg, zs, ya, za, gs, ga, *weights)


def _tile(n, target):
    t = min(n, target)
    assert n % t == 0, (n, target)
    return t


def kernel(x, pre_norm_g, post_norm_g, w_in, ssm_a_re, ssm_a_im, ssm_log_dt, ssm_b_re, ssm_b_im,
           ssm_c_re, ssm_c_im, ssm_d, w_glu, b_glu, w_branch_ssm, w_branch_attn, w_out):
    nb, l, d = x.shape
    depth = w_in.shape[0]
    assert d == D_MODEL and w_in.shape[2] == IN_COLS
    t = nb * l
    tm = _tile(l, 1024)
    lc = _tile(l, 512)
    tb = _tile(l, 256)

    abar_re, abar_im, wb, wc = _ssm_params(
        ssm_a_re, ssm_a_im, ssm_log_dt[..., None],
        ssm_b_re.transpose(0, 1, 3, 2), ssm_b_im.transpose(0, 1, 3, 2),
        ssm_c_re, ssm_c_im)
    ar = abar_re.reshape(depth, 2, SUBLANES, LANES)
    ai = abar_im.reshape(depth, 2, SUBLANES, LANES)

    row = lambda a: a[:, None, :]
    pre_g, post_g, d_skip, b_glu_r = row(pre_norm_g), row(post_norm_g), row(ssm_d), row(b_glu)

    xt = x.reshape(t, d)
    for li in range(depth):
        u, zs, q, k, v, za, gs, ga = _in_proj(xt, pre_g, w_in, li, nb=nb, tm=tm)
        yg = _ssm(u, wb, ar, ai, wc, d_skip, li, lc=lc)
        ya = _attention(q, k, v, nb=nb, tb=tb)
        xt = _out_block(xt, yg, zs, ya, za, gs, ga,
                        w_glu, b_glu_r, w_branch_ssm, w_branch_attn, w_out, post_g, li, tm=tm)
    return xt.reshape(nb, l, d)
```

```python
import functools
import math

import jax
import jax.numpy as jnp
from jax import lax
from jax.experimental import pallas as pl
from jax.experimental.pallas import tpu as pltpu

F32 = jnp.float32
BF16 = jnp.bfloat16

D_MODEL = 1024
SSM_WIDTH = 512
SSM_GROUP = 16
SSM_GROUPS = 32
SSM_STATE = 64
ATTN_WIDTH = 512
HEAD_DIM = 64
EPS = 1e-6
IN_COLS = 5120

LANES = 128
SUBLANES = 8
N_SLABS = SSM_WIDTH // LANES
SLAB_STATE = (LANES // SSM_GROUP) * SSM_STATE
STATE_CHUNKS = SLAB_STATE // LANES
N_STATE_VREGS = 4
SCAN_UNROLL = 32
N_HEAD_PAIRS = ATTN_WIDTH // LANES

VMEM_LIMIT = 56 * 1024 * 1024

LOG2E = math.log2(math.e)
Q_SCALE = HEAD_DIM ** -0.5 * LOG2E
EXIT_LOG2 = 150.0
NO_BLOCK = 1e30


def _sigmoid(x):
    return 1.0 / (1.0 + jnp.exp(-x))


def _silu(x):
    return x * _sigmoid(x)


def _gelu_tanh(x):
    return 0.5 * x * (1.0 + jnp.tanh(math.sqrt(2.0 / math.pi) * (x + 0.044715 * (x * x * x))))


def _in_proj_kernel(x_ref, g_ref, w_ref, u_ref, zs_ref, q_ref, k_ref, v_ref, za_ref, gs_ref, ga_ref):
    x = x_ref[...]
    ms = jnp.mean(x * x, axis=-1, keepdims=True)
    h = (x * lax.rsqrt(ms + EPS) * g_ref[...]).astype(BF16)

    def proj(lo, hi):
        return jnp.dot(h, w_ref[:, lo:hi].astype(BF16), preferred_element_type=F32)

    gs_ref[...] = _sigmoid(proj(3072, 4096)).astype(BF16)
    ga_ref[...] = _sigmoid(proj(4096, 5120)).astype(BF16)
    zs_ref[...] = _silu(proj(512, 1024)).astype(BF16)
    za_ref[...] = _silu(proj(2560, 3072)).astype(BF16)
    for ref, lo, scale in ((q_ref, 1024, Q_SCALE), (k_ref, 1536, None), (v_ref, 2048, None)):
        y = proj(lo, lo + ATTN_WIDTH)
        if scale is not None:
            y = y * scale
        for hp in range(N_HEAD_PAIRS):
            ref[hp] = y[:, hp * LANES:(hp + 1) * LANES].astype(BF16)
    u_ref[...] = proj(0, 512).astype(BF16)


def _layer_spec(a, li):
    return pl.BlockSpec((None,) + a.shape[1:], lambda *_: (li,) + (0,) * (a.ndim - 1),
                        pipeline_mode=pl.Buffered(1))


def _in_proj(x, g, w, li, *, nb, tm):
    t = x.shape[0]
    npb = t // nb // tm
    row = lambda n: pl.BlockSpec((tm, n), lambda i: (i, 0))
    seq = pl.BlockSpec((None, tm, SSM_WIDTH), lambda i: (i // npb, i % npb, 0))
    full = lambda a: _layer_spec(a, li)
    heads = pl.BlockSpec((N_HEAD_PAIRS, tm, LANES), lambda i: (0, i, 0))
    heads_shape = jax.ShapeDtypeStruct((N_HEAD_PAIRS, t, LANES), BF16)
    flat = lambda n: jax.ShapeDtypeStruct((t, n), BF16)
    return pl.pallas_call(
        _in_proj_kernel,
        grid=(t // tm,),
        in_specs=[row(D_MODEL), full(g), full(w)],
        out_specs=[seq, row(512), heads, heads, heads, row(512), row(1024), row(1024)],
        out_shape=[jax.ShapeDtypeStruct((nb, t // nb, SSM_WIDTH), BF16), flat(512),
                   heads_shape, heads_shape, heads_shape, flat(512), flat(1024), flat(1024)],
        compiler_params=pltpu.CompilerParams(
            dimension_semantics=("arbitrary",), vmem_limit_bytes=VMEM_LIMIT),
        name="in_proj",
    )(x, g, w)


def _ssm_param_kernel(are_ref, aim_ref, ldt_ref, bre_ref, bim_ref, cre_ref, cim_ref,
                      abr_ref, abi_ref, wb_ref, wc_ref):
    a_re = are_ref[0]
    a_im = aim_ref[0]
    dt = jnp.exp(ldt_ref[0])
    mag = jnp.exp(a_re * dt)
    abar_re = mag * jnp.cos(a_im * dt)
    abar_im = mag * jnp.sin(a_im * dt)
    nr = abar_re - 1.0
    ni = abar_im
    den = a_re * a_re + a_im * a_im
    f_re = ((nr * a_re + ni * a_im) / den)[:, None, :]
    f_im = ((ni * a_re - nr * a_im) / den)[:, None, :]
    abr_ref[0] = abar_re
    abi_ref[0] = abar_im
    b_re = bre_ref[0]
    b_im = bim_ref[0]
    bb_re = f_re * b_re - f_im * b_im
    bb_im = f_re * b_im + f_im * b_re

    gps = LANES // SSM_GROUP
    iota = lambda shape, dim: lax.broadcasted_iota(jnp.int32, shape, dim)
    spread = (iota((SSM_STATE, SLAB_STATE), 1) % SSM_STATE == iota((SSM_STATE, SLAB_STATE), 0)).astype(BF16)
    stack = (iota((SLAB_STATE, SSM_STATE), 0) % SSM_STATE == iota((SLAB_STATE, SSM_STATE), 1)).astype(BF16)
    keep_b = iota((LANES, SLAB_STATE), 0) // SSM_GROUP == iota((LANES, SLAB_STATE), 1) // SSM_STATE
    keep_c = iota((SLAB_STATE, LANES), 0) // SSM_STATE == iota((SLAB_STATE, LANES), 1) // SSM_GROUP

    def slab_rows(w, s):
        return w[s * gps:(s + 1) * gps].reshape(LANES, SSM_STATE).astype(BF16)

    for s in range(N_SLABS):
        for part, w in enumerate((bb_re, bb_im)):
            wide = jnp.dot(slab_rows(w, s), spread, preferred_element_type=F32)
            wb_ref[0, s, :, part * SLAB_STATE:(part + 1) * SLAB_STATE] = jnp.where(keep_b, wide, 0.0).astype(BF16)
        for part, (w, sign) in enumerate(((cre_ref[0], 1.0), (cim_ref[0], -1.0))):
            tall = lax.dot_general(stack, slab_rows(w, s), (((1,), (1,)), ((), ())),
                                   preferred_element_type=F32)
            wc_ref[0, s, part * SLAB_STATE:(part + 1) * SLAB_STATE, :] = jnp.where(keep_c, sign * tall, 0.0).astype(BF16)


def _ssm_params(a_re, a_im, log_dt, b_re_t, b_im_t, c_re, c_im):
    depth = a_re.shape[0]
    spec = lambda shape: pl.BlockSpec((1,) + shape[1:], lambda l: (l,) + (0,) * (len(shape) - 1))
    ins = (a_re, a_im, log_dt, b_re_t, b_im_t, c_re, c_im)
    out_shapes = [jax.ShapeDtypeStruct(a_re.shape, F32), jax.ShapeDtypeStruct(a_re.shape, F32),
                  jax.ShapeDtypeStruct((depth, N_SLABS, LANES, 2 * SLAB_STATE), BF16),
                  jax.ShapeDtypeStruct((depth, N_SLABS, 2 * SLAB_STATE, LANES), BF16)]
    return pl.pallas_call(
        _ssm_param_kernel,
        grid=(depth,),
        in_specs=[spec(a.shape) for a in ins],
        out_specs=[spec(o.shape) for o in out_shapes],
        out_shape=out_shapes,
        name="ssm_params",
    )(*ins)


def _ssm_kernel(u_ref, wb_ref, ar_ref, ai_ref, wc_ref, d_ref, y_ref, bu_scr, x_scr, st_scr, *, nb, lc):
    @pl.when(pl.program_id(0) == 0)
    def _():
        st_scr[...] = jnp.zeros_like(st_scr)

    nt = lc // SUBLANES

    for b in range(nb):
        for s in range(N_SLABS):
            bu = jnp.dot(u_ref[b, :, s * LANES:(s + 1) * LANES], wb_ref[s], preferred_element_type=F32)
            vp, base = s // 2, (s % 2) * STATE_CHUNKS
            for c in range(STATE_CHUNKS):
                j = base + c
                re = bu[:, c * LANES:(c + 1) * LANES]
                im = bu[:, SLAB_STATE + c * LANES:SLAB_STATE + (c + 1) * LANES]
                bu_scr[b, 2 * vp, :, j * SUBLANES:(j + 1) * SUBLANES, :] = re.reshape(nt, SUBLANES, LANES)
                bu_scr[b, 2 * vp + 1, :, j * SUBLANES:(j + 1) * SUBLANES, :] = im.reshape(nt, SUBLANES, LANES)

    a_r = [ar_ref[0], ar_ref[1]]
    a_i = [ai_ref[0], ai_ref[1]]

    def steps(it, carry):
        xs = list(carry)
        for t in range(SCAN_UNROLL):
            i, r = it * (SCAN_UNROLL // SUBLANES) + t // SUBLANES, t % SUBLANES
            row = pl.multiple_of(i * (SUBLANES * SUBLANES) + r * SUBLANES, SUBLANES)
            for b in range(nb):
                for vp in range(2):
                    k = (b * 2 + vp) * 2
                    b_r = bu_scr[b, 2 * vp, i, pl.ds(r, SUBLANES, stride=SUBLANES), :]
                    b_i = bu_scr[b, 2 * vp + 1, i, pl.ds(r, SUBLANES, stride=SUBLANES), :]
                    x_r, x_i = xs[k], xs[k + 1]
                    n_r = a_r[vp] * x_r - a_i[vp] * x_i + b_r
                    n_i = a_r[vp] * x_i + a_i[vp] * x_r + b_i
                    x_scr[b, 2 * vp, pl.ds(row, SUBLANES), :] = n_r
                    x_scr[b, 2 * vp + 1, pl.ds(row, SUBLANES), :] = n_i
                    xs[k], xs[k + 1] = n_r, n_i
        return tuple(xs)

    init = tuple(st_scr[b, v] for b in range(nb) for v in range(N_STATE_VREGS))
    final = lax.fori_loop(0, lc // SCAN_UNROLL, steps, init)
    for b in range(nb):
        for v in range(N_STATE_VREGS):
            st_scr[b, v] = final[b * N_STATE_VREGS + v]

    for b in range(nb):
        for s in range(N_SLABS):
            vp, base = s // 2, (s % 2) * STATE_CHUNKS
            parts = [x_scr[b, 2 * vp + ri, pl.ds(base + c, lc, stride=SUBLANES), :]
                     for ri in range(2) for c in range(STATE_CHUNKS)]
            xs = jnp.concatenate(parts, axis=1).astype(BF16)
            y = jnp.dot(xs, wc_ref[s], preferred_element_type=F32)
            sl = slice(s * LANES, (s + 1) * LANES)
            y = y + d_ref[:, sl] * u_ref[b, :, sl].astype(F32)
            y_ref[b, :, sl] = _gelu_tanh(y).astype(BF16)


def _ssm(u, wb, ar, ai, wc, d, li, *, lc):
    nb, l, _ = u.shape
    full = lambda a: _layer_spec(a, li)
    blk = pl.BlockSpec((nb, lc, SSM_WIDTH), lambda c: (0, c, 0))
    return pl.pallas_call(
        functools.partial(_ssm_kernel, nb=nb, lc=lc),
        grid=(l // lc,),
        in_specs=[blk, full(wb), full(ar), full(ai), full(wc), full(d)],
        out_specs=blk,
        out_shape=jax.ShapeDtypeStruct(u.shape, BF16),
        scratch_shapes=[
            pltpu.VMEM((nb, N_STATE_VREGS, lc // SUBLANES, SUBLANES * SUBLANES, LANES), F32),
            pltpu.VMEM((nb, N_STATE_VREGS, lc * SUBLANES, LANES), F32),
            pltpu.VMEM((nb, N_STATE_VREGS, SUBLANES, LANES), F32),
        ],
        compiler_params=pltpu.CompilerParams(
            dimension_semantics=("arbitrary",), vmem_limit_bytes=VMEM_LIMIT),
        name="ssm",
    )(u, wb, ar, ai, wc, d)


def _softplus2(z):
    return jnp.maximum(z, 0.0) + jnp.log2(1.0 + jnp.exp2(-jnp.abs(z)))


def _attn_kernel(q_ref, kd_ref, kp_ref, vd_ref, vp_ref, k_hbm, v_hbm, tri_ref, o_ref,
                 acc_scr, carry_scr, kv_buf, kv_sem, *, tb, seq_len):
    qi = pl.program_id(1)
    low = lax.broadcasted_iota(jnp.int32, (tb, LANES), 1) < HEAD_DIM
    causal = (lax.broadcasted_iota(jnp.int32, (tb, tb), 1)
              < lax.broadcasted_iota(jnp.int32, (tb, tb), 0))
    tri = tri_ref[...]
    no_prev = jnp.where(qi > 0, 0.0, NO_BLOCK)

    causal2 = jnp.concatenate([causal, causal], axis=0)
    half = tb // 2
    first = tb // 4

    def stacked_q(hp):
        q = q_ref[hp]
        zero = jnp.zeros_like(q)
        return jnp.concatenate([jnp.where(low, q, zero), jnp.where(low, zero, q)], axis=0)

    def scores(q2, k):
        return lax.dot_general(q2, k, (((1,), (1,)), ((), ())), preferred_element_type=F32)

    def unstack(x):
        return jnp.where(low, x[:tb], x[tb:])

    def near_prev(f, *tiles, dtype):
        rows = []
        for h in range(2):
            top = slice(h * tb, h * tb + first)
            bot = slice(h * tb + first, (h + 1) * tb)
            rows.append(f(*(t[top, :] for t in tiles)).astype(dtype))
            rows.append(jnp.concatenate([jnp.zeros((tb - first, half), dtype),
                                         f(*(t[bot, half:] for t in tiles)).astype(dtype)], axis=1))
        return jnp.concatenate(rows, axis=0)

    def stage_a(hp):
        q2 = stacked_q(hp)
        z0 = scores(q2, kd_ref[hp])
        z1 = scores(q2, kp_ref[hp])
        sp = jnp.concatenate([jnp.where(causal2, _softplus2(z0), 0.0).astype(BF16),
                              near_prev(_softplus2, z1, dtype=BF16)], axis=0)
        return z0, z1, sp

    def stage_b(hp, z0, z1, sp):
        cum = jnp.dot(sp, tri, preferred_element_type=F32)
        cum0, cum1 = cum[:2 * tb], cum[2 * tb:]
        c0 = cum0[:, 0:1]
        w0 = jnp.where(causal2, jnp.exp2(z0 - cum0), 0.0).astype(BF16)
        c0p = jnp.broadcast_to(c0 + no_prev, (2 * tb, tb))
        w1 = near_prev(lambda z, c, p: jnp.exp2(z - c - p), z1, cum1, c0p, dtype=BF16)
        return w0, w1, c0 + cum1[:, 0:1]

    def stage_c(hp, w0, w1, carry, min_carry):
        out = unstack(jnp.dot(w0, vd_ref[hp], preferred_element_type=F32)
                      + jnp.dot(w1, vp_ref[hp], preferred_element_type=F32))
        carry_scr[hp] = carry
        acc_scr[hp] = out
        o_ref[hp] = out.astype(o_ref.dtype)
        return jnp.minimum(min_carry, jnp.minimum(carry[:tb], carry[tb:]))

    min_carry = jnp.full((tb, 1), jnp.inf, F32)
    a_out, b_out = {}, {}
    for step in range(N_HEAD_PAIRS + 2):
        if step < N_HEAD_PAIRS:
            a_out[step] = stage_a(step)
        if 0 <= step - 1 < N_HEAD_PAIRS:
            b_out[step - 1] = stage_b(step - 1, *a_out.pop(step - 1))
        if 0 <= step - 2 < N_HEAD_PAIRS:
            min_carry = stage_c(step - 2, *b_out.pop(step - 2), min_carry)

    @pl.when(jnp.logical_and(qi >= 1, jnp.min(min_carry) < EXIT_LOG2))
    def _():
        row_in_block = lax.broadcasted_iota(jnp.int32, (2 * tb, tb), 0) % tb
        skipped = jnp.logical_and(row_in_block >= first,
                                  lax.broadcasted_iota(jnp.int32, (2 * tb, tb), 1) < half)

        def far(hp, _):
            q2 = stacked_q(hp)

            z = scores(q2, kp_ref[hp])
            sp = jnp.where(skipped, _softplus2(z), 0.0)
            cum = jnp.dot(sp.astype(BF16), tri, preferred_element_type=F32)
            carry = carry_scr[hp]
            w = jnp.where(skipped, jnp.exp2(z - cum - carry), 0.0)
            acc0 = jnp.dot(w.astype(BF16), vp_ref[hp], preferred_element_type=F32)
            carry0 = carry + cum[:, 0:1]

            def fetch(j):
                rows = pl.ds(pl.multiple_of(pl.program_id(0) * seq_len + j * tb, tb), tb)
                copies = [pltpu.make_async_copy(src.at[hp, rows, :], kv_buf.at[n], kv_sem.at[n])
                          for n, src in enumerate((k_hbm, v_hbm))]
                for c in copies:
                    c.start()
                for c in copies:
                    c.wait()

            def cond(st):
                j, carry, _ = st
                return jnp.logical_and(j >= 0, jnp.min(carry) < EXIT_LOG2)

            def body(st):
                j, carry, acc = st
                fetch(j)
                z = scores(q2, kv_buf[0])
                cum = jnp.dot(_softplus2(z).astype(BF16), tri, preferred_element_type=F32)
                w = jnp.exp2(z - cum - carry).astype(BF16)
                return j - 1, carry + cum[:, 0:1], acc + jnp.dot(w, kv_buf[1], preferred_element_type=F32)

            init = (qi - 2, carry0, acc0)
            acc = lax.while_loop(cond, body, init)[2]
            o_ref[hp] = (acc_scr[hp] + unstack(acc)).astype(o_ref.dtype)
            return 0

        lax.fori_loop(0, N_HEAD_PAIRS, far, 0)


def _attention(q, k, v, *, nb, tb):
    nhp, t, _ = q.shape
    l = t // nb
    nq = l // tb
    tri = (lax.broadcasted_iota(jnp.int32, (tb, tb), 0) >= lax.broadcasted_iota(jnp.int32, (tb, tb), 1)).astype(BF16)
    qspec = pl.BlockSpec((nhp, tb, LANES), lambda b, i: (0, b * nq + i, 0))
    prev = pl.BlockSpec((nhp, tb, LANES), lambda b, i: (0, b * nq + jnp.maximum(i - 1, 0), 0))
    hbm = pl.BlockSpec(memory_space=pl.ANY)
    return pl.pallas_call(
        functools.partial(_attn_kernel, tb=tb, seq_len=l),
        grid=(nb, nq),
        in_specs=[qspec, qspec, prev, qspec, prev, hbm, hbm, pl.BlockSpec((tb, tb), lambda b, i: (0, 0))],
        out_specs=qspec,
        out_shape=jax.ShapeDtypeStruct(q.shape, BF16),
        scratch_shapes=[pltpu.VMEM((nhp, tb, LANES), F32), pltpu.VMEM((nhp, 2 * tb, 1), F32),
                        pltpu.VMEM((2, tb, LANES), BF16), pltpu.SemaphoreType.DMA((2,))],
        compiler_params=pltpu.CompilerParams(
            dimension_semantics=("arbitrary", "arbitrary"), vmem_limit_bytes=VMEM_LIMIT),
        name="attention",
    )(q, k, k, v, v, k, v, tri)


def _out_kernel(x_ref, yg_ref, zs_ref, ya_ref, za_ref, gs_ref, ga_ref,
                wglu_ref, bglu_ref, wbs_ref, wba_ref, wout_ref, g_ref, o_ref):
    gl = jnp.dot(yg_ref[...], wglu_ref[...].astype(BF16), preferred_element_type=F32) + bglu_ref[...]
    ys = gl[:, :SSM_WIDTH] * _sigmoid(gl[:, SSM_WIDTH:]) * zs_ref[...].astype(F32)
    ya = jnp.concatenate([ya_ref[hp] for hp in range(N_HEAD_PAIRS)], axis=1).astype(F32) * za_ref[...].astype(F32)
    merged = (gs_ref[...].astype(F32) * jnp.dot(ys.astype(BF16), wbs_ref[...].astype(BF16), preferred_element_type=F32)
              + ga_ref[...].astype(F32) * jnp.dot(ya.astype(BF16), wba_ref[...].astype(BF16), preferred_element_type=F32))
    out = jnp.dot(merged.astype(BF16), wout_ref[...].astype(BF16), preferred_element_type=F32)
    ms = jnp.mean(out * out, axis=-1, keepdims=True)
    o_ref[...] = x_ref[...] + out * lax.rsqrt(ms + EPS) * g_ref[...]


def _out_block(x, yg, zs, ya, za, gs, ga, wglu, bglu, wbs, wba, wout, g, li, *, tm):
    t = x.shape[0]
    npb = yg.shape[1] // tm
    row = lambda n: pl.BlockSpec((tm, n), lambda i: (i, 0))
    seq = pl.BlockSpec((None, tm, SSM_WIDTH), lambda i: (i // npb, i % npb, 0))
    full = lambda a: _layer_spec(a, li)
    weights = (wglu, bglu, wbs, wba, wout, g)
    return pl.pallas_call(
        _out_kernel,
        grid=(t // tm,),
        in_specs=[row(D_MODEL), seq, row(512),
                  pl.BlockSpec((N_HEAD_PAIRS, tm, LANES), lambda i: (0, i, 0)),
                  row(512), row(D_MODEL), row(D_MODEL)] + [full(a) for a in weights],
        out_specs=row(D_MODEL),
        out_shape=jax.ShapeDtypeStruct(x.shape, F32),
        compiler_params=pltpu.CompilerParams(
            dimension_semantics=("arbitrary",), vmem_limit_bytes=VMEM_LIMIT),
        name="out_block",
    )(x, yg, zs, ya, za, gs, ga, *weights)


def _tile(n, target):
    t = min(n, target)
    assert n % t == 0, (n, target)
    return t


def kernel(x, pre_norm_g, post_norm_g, w_in, ssm_a_re, ssm_a_im, ssm_log_dt, ssm_b_re, ssm_b_im,
           ssm_c_re, ssm_c_im, ssm_d, w_glu, b_glu, w_branch_ssm, w_branch_attn, w_out):
    nb, l, d = x.shape
    depth = w_in.shape[0]
    assert d == D_MODEL and w_in.shape[2] == IN_COLS
    t = nb * l
    tm = _tile(l, 1024)
    lc = _tile(l, 512)
    tb = _tile(l, 256)

    abar_re, abar_im, wb, wc = _ssm_params(
        ssm_a_re, ssm_a_im, ssm_log_dt[..., None],
        ssm_b_re.transpose(0, 1, 3, 2), ssm_b_im.transpose(0, 1, 3, 2),
        ssm_c_re, ssm_c_im)
    ar = abar_re.reshape(depth, 2, SUBLANES, LANES)
    ai = abar_im.reshape(depth, 2, SUBLANES, LANES)

    row = lambda a: a[:, None, :]
    pre_g, post_g, d_skip, b_glu_r = row(pre_norm_g), row(post_norm_g), row(ssm_d), row(b_glu)

    xt = x.reshape(t, d)
    for li in range(depth):
        u, zs, q, k, v, za, gs, ga = _in_proj(xt, pre_g, w_in, li, nb=nb, tm=tm)
        yg = _ssm(u, wb, ar, ai, wc, d_skip, li, lc=lc)
        ya = _attention(q, k, v, nb=nb, tb=tb)
        xt = _out_block(xt, yg, zs, ya, za, gs, ga,
                        w_glu, b_glu_r, w_branch_ssm, w_branch_attn, w_out, post_g, li, tm=tm)
    return xt.reshape(nb, l, d)
```

```python
import functools
import math

import jax
import jax.numpy as jnp
from jax import lax
from jax.experimental import pallas as pl
from jax.experimental.pallas import tpu as pltpu

F32 = jnp.float32
BF16 = jnp.bfloat16

D_MODEL = 1024
SSM_WIDTH = 512
SSM_GROUP = 16
SSM_GROUPS = 32
SSM_STATE = 64
ATTN_WIDTH = 512
HEAD_DIM = 64
EPS = 1e-6
IN_COLS = 5120

LANES = 128
SUBLANES = 8
N_SLABS = SSM_WIDTH // LANES
SLAB_STATE = (LANES // SSM_GROUP) * SSM_STATE
STATE_CHUNKS = SLAB_STATE // LANES
N_STATE_VREGS = 4
SCAN_UNROLL = 32
N_HEAD_PAIRS = ATTN_WIDTH // LANES

VMEM_LIMIT = 56 * 1024 * 1024

LOG2E = math.log2(math.e)
Q_SCALE = HEAD_DIM ** -0.5 * LOG2E
EXIT_LOG2 = 150.0
NO_BLOCK = 1e30


def _sigmoid(x):
    return 1.0 / (1.0 + jnp.exp(-x))


def _silu(x):
    return x * _sigmoid(x)


def _gelu_tanh(x):
    return 0.5 * x * (1.0 + jnp.tanh(math.sqrt(2.0 / math.pi) * (x + 0.044715 * (x * x * x))))


IN_PROJ_ORDER = ((3072, 4096), (4096, 5120), (512, 1024), (2560, 3072), (1024, 1536), (1536, 2048),
                 (2048, 2560), (0, 512))


def _in_proj_kernel(x_ref, g_ref, w_hbm, u_ref, zs_ref, q_ref, k_ref, v_ref, za_ref, gs_ref, ga_ref,
                    w_buf, w_sem, *, li):
    first_step = pl.program_id(0) == 0

    def weight_copy(n):
        lo, hi = IN_PROJ_ORDER[n]
        return pltpu.make_async_copy(w_hbm.at[li, :, lo:hi], w_buf.at[:, lo:hi], w_sem.at[n])

    @pl.when(first_step)
    def _():
        for n in range(len(IN_PROJ_ORDER)):
            weight_copy(n).start()

    def project(wait):
        x = x_ref[...]
        ms = jnp.mean(x * x, axis=-1, keepdims=True)
        h = (x * lax.rsqrt(ms + EPS) * g_ref[...]).astype(BF16)

        def proj(n):
            if wait:
                weight_copy(n).wait()
            lo, hi = IN_PROJ_ORDER[n]
            return jnp.dot(h, w_buf[:, lo:hi].astype(BF16), preferred_element_type=F32)

        gs_ref[...] = _sigmoid(proj(0)).astype(BF16)
        ga_ref[...] = _sigmoid(proj(1)).astype(BF16)
        zs_ref[...] = _silu(proj(2)).astype(BF16)
        za_ref[...] = _silu(proj(3)).astype(BF16)
        for n, ref, scale in ((4, q_ref, Q_SCALE), (5, k_ref, None), (6, v_ref, None)):
            y = proj(n)
            if scale is not None:
                y = y * scale
            for hp in range(N_HEAD_PAIRS):
                ref[hp] = y[:, hp * LANES:(hp + 1) * LANES].astype(BF16)
        u_ref[...] = proj(7).astype(BF16)

    @pl.when(first_step)
    def _():
        project(wait=True)

    @pl.when(jnp.logical_not(first_step))
    def _():
        project(wait=False)


def _layer_spec(a, li):
    return pl.BlockSpec((None,) + a.shape[1:], lambda *_: (li,) + (0,) * (a.ndim - 1),
                        pipeline_mode=pl.Buffered(1))


def _in_proj(x, g, w, li, *, nb, tm):
    t = x.shape[0]
    npb = t // nb // tm
    row = lambda n: pl.BlockSpec((tm, n), lambda i: (i, 0))
    seq = pl.BlockSpec((None, tm, SSM_WIDTH), lambda i: (i // npb, i % npb, 0))
    full = lambda a: _layer_spec(a, li)
    heads = pl.BlockSpec((N_HEAD_PAIRS, tm, LANES), lambda i: (0, i, 0))
    heads_shape = jax.ShapeDtypeStruct((N_HEAD_PAIRS, t, LANES), BF16)
    flat = lambda n: jax.ShapeDtypeStruct((t, n), BF16)
    return pl.pallas_call(
        functools.partial(_in_proj_kernel, li=li),
        grid=(t // tm,),
        in_specs=[row(D_MODEL), full(g), pl.BlockSpec(memory_space=pl.ANY)],
        out_specs=[seq, row(512), heads, heads, heads, row(512), row(1024), row(1024)],
        out_shape=[jax.ShapeDtypeStruct((nb, t // nb, SSM_WIDTH), BF16), flat(512),
                   heads_shape, heads_shape, heads_shape, flat(512), flat(1024), flat(1024)],
        scratch_shapes=[pltpu.VMEM(w.shape[1:], F32), pltpu.SemaphoreType.DMA((len(IN_PROJ_ORDER),))],
        compiler_params=pltpu.CompilerParams(
            dimension_semantics=("arbitrary",), vmem_limit_bytes=VMEM_LIMIT),
        name="in_proj",
    )(x, g, w)


def _ssm_param_kernel(are_ref, aim_ref, ldt_ref, bre_ref, bim_ref, cre_ref, cim_ref,
                      abr_ref, abi_ref, wb_ref, wc_ref):
    a_re = are_ref[0]
    a_im = aim_ref[0]
    dt = jnp.exp(ldt_ref[0])
    mag = jnp.exp(a_re * dt)
    abar_re = mag * jnp.cos(a_im * dt)
    abar_im = mag * jnp.sin(a_im * dt)
    nr = abar_re - 1.0
    ni = abar_im
    den = a_re * a_re + a_im * a_im
    f_re = ((nr * a_re + ni * a_im) / den)[:, None, :]
    f_im = ((ni * a_re - nr * a_im) / den)[:, None, :]
    abr_ref[0] = abar_re
    abi_ref[0] = abar_im
    b_re = bre_ref[0]
    b_im = bim_ref[0]
    bb_re = f_re * b_re - f_im * b_im
    bb_im = f_re * b_im + f_im * b_re

    gps = LANES // SSM_GROUP
    iota = lambda shape, dim: lax.broadcasted_iota(jnp.int32, shape, dim)
    spread = (iota((SSM_STATE, SLAB_STATE), 1) % SSM_STATE == iota((SSM_STATE, SLAB_STATE), 0)).astype(BF16)
    stack = (iota((SLAB_STATE, SSM_STATE), 0) % SSM_STATE == iota((SLAB_STATE, SSM_STATE), 1)).astype(BF16)
    keep_b = iota((LANES, SLAB_STATE), 0) // SSM_GROUP == iota((LANES, SLAB_STATE), 1) // SSM_STATE
    keep_c = iota((SLAB_STATE, LANES), 0) // SSM_STATE == iota((SLAB_STATE, LANES), 1) // SSM_GROUP

    def slab_rows(w, s):
        return w[s * gps:(s + 1) * gps].reshape(LANES, SSM_STATE).astype(BF16)

    for s in range(N_SLABS):
        for part, w in enumerate((bb_re, bb_im)):
            wide = jnp.dot(slab_rows(w, s), spread, preferred_element_type=F32)
            wb_ref[0, s, :, part * SLAB_STATE:(part + 1) * SLAB_STATE] = jnp.where(keep_b, wide, 0.0).astype(BF16)
        for part, (w, sign) in enumerate(((cre_ref[0], 1.0), (cim_ref[0], -1.0))):
            tall = lax.dot_general(stack, slab_rows(w, s), (((1,), (1,)), ((), ())),
                                   preferred_element_type=F32)
            wc_ref[0, s, part * SLAB_STATE:(part + 1) * SLAB_STATE, :] = jnp.where(keep_c, sign * tall, 0.0).astype(BF16)


def _ssm_params(a_re, a_im, log_dt, b_re_t, b_im_t, c_re, c_im):
    depth = a_re.shape[0]
    spec = lambda shape: pl.BlockSpec((1,) + shape[1:], lambda l: (l,) + (0,) * (len(shape) - 1))
    ins = (a_re, a_im, log_dt, b_re_t, b_im_t, c_re, c_im)
    out_shapes = [jax.ShapeDtypeStruct(a_re.shape, F32), jax.ShapeDtypeStruct(a_re.shape, F32),
                  jax.ShapeDtypeStruct((depth, N_SLABS, LANES, 2 * SLAB_STATE), BF16),
                  jax.ShapeDtypeStruct((depth, N_SLABS, 2 * SLAB_STATE, LANES), BF16)]
    return pl.pallas_call(
        _ssm_param_kernel,
        grid=(depth,),
        in_specs=[spec(a.shape) for a in ins],
        out_specs=[spec(o.shape) for o in out_shapes],
        out_shape=out_shapes,
        name="ssm_params",
    )(*ins)


def _ssm_kernel(u_ref, wb_ref, ar_ref, ai_ref, wc_ref, d_ref, y_ref, bu_scr, x_scr, st_scr, *, nb, lc):
    @pl.when(pl.program_id(0) == 0)
    def _():
        st_scr[...] = jnp.zeros_like(st_scr)

    nt = lc // SUBLANES

    for b in range(nb):
        for s in range(N_SLABS):
            bu = jnp.dot(u_ref[b, :, s * LANES:(s + 1) * LANES], wb_ref[s], preferred_element_type=F32)
            vp, base = s // 2, (s % 2) * STATE_CHUNKS
            for c in range(STATE_CHUNKS):
                j = base + c
                re = bu[:, c * LANES:(c + 1) * LANES]
                im = bu[:, SLAB_STATE + c * LANES:SLAB_STATE + (c + 1) * LANES]
                bu_scr[b, 2 * vp, :, j * SUBLANES:(j + 1) * SUBLANES, :] = re.reshape(nt, SUBLANES, LANES)
                bu_scr[b, 2 * vp + 1, :, j * SUBLANES:(j + 1) * SUBLANES, :] = im.reshape(nt, SUBLANES, LANES)

    a_r = [ar_ref[0], ar_ref[1]]
    a_i = [ai_ref[0], ai_ref[1]]

    def steps(it, carry):
        xs = list(carry)
        for t in range(SCAN_UNROLL):
            i, r = it * (SCAN_UNROLL // SUBLANES) + t // SUBLANES, t % SUBLANES
            row = pl.multiple_of(i * (SUBLANES * SUBLANES) + r * SUBLANES, SUBLANES)
            for b in range(nb):
                for vp in range(2):
                    k = (b * 2 + vp) * 2
                    b_r = bu_scr[b, 2 * vp, i, pl.ds(r, SUBLANES, stride=SUBLANES), :]
                    b_i = bu_scr[b, 2 * vp + 1, i, pl.ds(r, SUBLANES, stride=SUBLANES), :]
                    x_r, x_i = xs[k], xs[k + 1]
                    n_r = a_r[vp] * x_r - a_i[vp] * x_i + b_r
                    n_i = a_r[vp] * x_i + a_i[vp] * x_r + b_i
                    x_scr[b, 2 * vp, pl.ds(row, SUBLANES), :] = n_r
                    x_scr[b, 2 * vp + 1, pl.ds(row, SUBLANES), :] = n_i
                    xs[k], xs[k + 1] = n_r, n_i
        return tuple(xs)

    init = tuple(st_scr[b, v] for b in range(nb) for v in range(N_STATE_VREGS))
    final = lax.fori_loop(0, lc // SCAN_UNROLL, steps, init)
    for b in range(nb):
        for v in range(N_STATE_VREGS):
            st_scr[b, v] = final[b * N_STATE_VREGS + v]

    for b in range(nb):
        for s in range(N_SLABS):
            vp, base = s // 2, (s % 2) * STATE_CHUNKS
            parts = [x_scr[b, 2 * vp + ri, pl.ds(base + c, lc, stride=SUBLANES), :]
                     for ri in range(2) for c in range(STATE_CHUNKS)]
            xs = jnp.concatenate(parts, axis=1).astype(BF16)
            y = jnp.dot(xs, wc_ref[s], preferred_element_type=F32)
            sl = slice(s * LANES, (s + 1) * LANES)
            y = y + d_ref[:, sl] * u_ref[b, :, sl].astype(F32)
            y_ref[b, :, sl] = _gelu_tanh(y).astype(BF16)


def _ssm(u, wb, ar, ai, wc, d, li, *, lc):
    nb, l, _ = u.shape
    full = lambda a: _layer_spec(a, li)
    blk = pl.BlockSpec((nb, lc, SSM_WIDTH), lambda c: (0, c, 0))
    return pl.pallas_call(
        functools.partial(_ssm_kernel, nb=nb, lc=lc),
        grid=(l // lc,),
        in_specs=[blk, full(wb), full(ar), full(ai), full(wc), full(d)],
        out_specs=blk,
        out_shape=jax.ShapeDtypeStruct(u.shape, BF16),
        scratch_shapes=[
            pltpu.VMEM((nb, N_STATE_VREGS, lc // SUBLANES, SUBLANES * SUBLANES, LANES), F32),
            pltpu.VMEM((nb, N_STATE_VREGS, lc * SUBLANES, LANES), F32),
            pltpu.VMEM((nb, N_STATE_VREGS, SUBLANES, LANES), F32),
        ],
        compiler_params=pltpu.CompilerParams(
            dimension_semantics=("arbitrary",), vmem_limit_bytes=VMEM_LIMIT),
        name="ssm",
    )(u, wb, ar, ai, wc, d)


def _softplus2(z):
    return jnp.maximum(z, 0.0) + jnp.log2(1.0 + jnp.exp2(-jnp.abs(z)))


def _attn_kernel(q_ref, kd_ref, kp_ref, vd_ref, vp_ref, k_hbm, v_hbm, tri_ref, o_ref,
                 acc_scr, carry_scr, kv_buf, kv_sem, *, tb, seq_len):
    qi = pl.program_id(1)
    low = lax.broadcasted_iota(jnp.int32, (tb, LANES), 1) < HEAD_DIM
    causal = (lax.broadcasted_iota(jnp.int32, (tb, tb), 1)
              < lax.broadcasted_iota(jnp.int32, (tb, tb), 0))
    tri = tri_ref[...]
    no_prev = jnp.where(qi > 0, 0.0, NO_BLOCK)

    causal2 = jnp.concatenate([causal, causal], axis=0)
    half = tb // 2
    first = tb // 4

    def stacked_q(hp):
        q = q_ref[hp]
        zero = jnp.zeros_like(q)
        return jnp.concatenate([jnp.where(low, q, zero), jnp.where(low, zero, q)], axis=0)

    def scores(q2, k):
        return lax.dot_general(q2, k, (((1,), (1,)), ((), ())), preferred_element_type=F32)

    def unstack(x):
        return jnp.where(low, x[:tb], x[tb:])

    def near_prev(f, *tiles, dtype):
        rows = []
        for h in range(2):
            top = slice(h * tb, h * tb + first)
            bot = slice(h * tb + first, (h + 1) * tb)
            rows.append(f(*(t[top, :] for t in tiles)).astype(dtype))
            rows.append(jnp.concatenate([jnp.zeros((tb - first, half), dtype),
                                         f(*(t[bot, half:] for t in tiles)).astype(dtype)], axis=1))
        return jnp.concatenate(rows, axis=0)

    def stage_a(hp):
        q2 = stacked_q(hp)
        z0 = scores(q2, kd_ref[hp])
        z1 = scores(q2, kp_ref[hp])
        sp = jnp.concatenate([jnp.where(causal2, _softplus2(z0), 0.0).astype(BF16),
                              near_prev(_softplus2, z1, dtype=BF16)], axis=0)
        return z0, z1, sp

    def stage_b(hp, z0, z1, sp):
        cum = jnp.dot(sp, tri, preferred_element_type=F32)
        cum0, cum1 = cum[:2 * tb], cum[2 * tb:]
        c0 = cum0[:, 0:1]
        w0 = jnp.where(causal2, jnp.exp2(z0 - cum0), 0.0).astype(BF16)
        c0p = jnp.broadcast_to(c0 + no_prev, (2 * tb, tb))
        w1 = near_prev(lambda z, c, p: jnp.exp2(z - c - p), z1, cum1, c0p, dtype=BF16)
        return w0, w1, c0 + cum1[:, 0:1]

    def stage_c(hp, w0, w1, carry, min_carry):
        out = unstack(jnp.dot(w0, vd_ref[hp], preferred_element_type=F32)
                      + jnp.dot(w1, vp_ref[hp], preferred_element_type=F32))
        carry_scr[hp] = carry
        acc_scr[hp] = out
        o_ref[hp] = out.astype(o_ref.dtype)
        return jnp.minimum(min_carry, jnp.minimum(carry[:tb], carry[tb:]))

    min_carry = jnp.full((tb, 1), jnp.inf, F32)
    a_out, b_out = {}, {}
    for step in range(N_HEAD_PAIRS + 2):
        if step < N_HEAD_PAIRS:
            a_out[step] = stage_a(step)
        if 0 <= step - 1 < N_HEAD_PAIRS:
            b_out[step - 1] = stage_b(step - 1, *a_out.pop(step - 1))
        if 0 <= step - 2 < N_HEAD_PAIRS:
            min_carry = stage_c(step - 2, *b_out.pop(step - 2), min_carry)

    @pl.when(jnp.logical_and(qi >= 1, jnp.min(min_carry) < EXIT_LOG2))
    def _():
        row_in_block = lax.broadcasted_iota(jnp.int32, (2 * tb, tb), 0) % tb
        skipped = jnp.logical_and(row_in_block >= first,
                                  lax.broadcasted_iota(jnp.int32, (2 * tb, tb), 1) < half)

        def far(hp, _):
            q2 = stacked_q(hp)

            z = scores(q2, kp_ref[hp])
            sp = jnp.where(skipped, _softplus2(z), 0.0)
            cum = jnp.dot(sp.astype(BF16), tri, preferred_element_type=F32)
            carry = carry_scr[hp]
            w = jnp.where(skipped, jnp.exp2(z - cum - carry), 0.0)
            acc0 = jnp.dot(w.astype(BF16), vp_ref[hp], preferred_element_type=F32)
            carry0 = carry + cum[:, 0:1]

            def fetch(j):
                rows = pl.ds(pl.multiple_of(pl.program_id(0) * seq_len + j * tb, tb), tb)
                copies = [pltpu.make_async_copy(src.at[hp, rows, :], kv_buf.at[n], kv_sem.at[n])
                          for n, src in enumerate((k_hbm, v_hbm))]
                for c in copies:
                    c.start()
                for c in copies:
                    c.wait()

            def cond(st):
                j, carry, _ = st
                return jnp.logical_and(j >= 0, jnp.min(carry) < EXIT_LOG2)

            def body(st):
                j, carry, acc = st
                fetch(j)
                z = scores(q2, kv_buf[0])
                cum = jnp.dot(_softplus2(z).astype(BF16), tri, preferred_element_type=F32)
                w = jnp.exp2(z - cum - carry).astype(BF16)
                return j - 1, carry + cum[:, 0:1], acc + jnp.dot(w, kv_buf[1], preferred_element_type=F32)

            init = (qi - 2, carry0, acc0)
            acc = lax.while_loop(cond, body, init)[2]
            o_ref[hp] = (acc_scr[hp] + unstack(acc)).astype(o_ref.dtype)
            return 0

        lax.fori_loop(0, N_HEAD_PAIRS, far, 0)


def _attention(q, k, v, *, nb, tb):
    nhp, t, _ = q.shape
    l = t // nb
    nq = l // tb
    tri = (lax.broadcasted_iota(jnp.int32, (tb, tb), 0) >= lax.broadcasted_iota(jnp.int32, (tb, tb), 1)).astype(BF16)
    qspec = pl.BlockSpec((nhp, tb, LANES), lambda b, i: (0, b * nq + i, 0))
    prev = pl.BlockSpec((nhp, tb, LANES), lambda b, i: (0, b * nq + jnp.maximum(i - 1, 0), 0))
    hbm = pl.BlockSpec(memory_space=pl.ANY)
    return pl.pallas_call(
        functools.partial(_attn_kernel, tb=tb, seq_len=l),
        grid=(nb, nq),
        in_specs=[qspec, qspec, prev, qspec, prev, hbm, hbm, pl.BlockSpec((tb, tb), lambda b, i: (0, 0))],
        out_specs=qspec,
        out_shape=jax.ShapeDtypeStruct(q.shape, BF16),
        scratch_shapes=[pltpu.VMEM((nhp, tb, LANES), F32), pltpu.VMEM((nhp, 2 * tb, 1), F32),
                        pltpu.VMEM((2, tb, LANES), BF16), pltpu.SemaphoreType.DMA((2,))],
        compiler_params=pltpu.CompilerParams(
            dimension_semantics=("arbitrary", "arbitrary"), vmem_limit_bytes=VMEM_LIMIT),
        name="attention",
    )(q, k, k, v, v, k, v, tri)


def _out_kernel(x_ref, yg_ref, zs_ref, ya_ref, za_ref, gs_ref, ga_ref,
                wglu_ref, bglu_ref, wbs_ref, wba_ref, wout_ref, g_ref, o_ref):
    gl = jnp.dot(yg_ref[...], wglu_ref[...].astype(BF16), preferred_element_type=F32) + bglu_ref[...]
    ys = gl[:, :SSM_WIDTH] * _sigmoid(gl[:, SSM_WIDTH:]) * zs_ref[...].astype(F32)
    ya = jnp.concatenate([ya_ref[hp] for hp in range(N_HEAD_PAIRS)], axis=1).astype(F32) * za_ref[...].astype(F32)
    merged = (gs_ref[...].astype(F32) * jnp.dot(ys.astype(BF16), wbs_ref[...].astype(BF16), preferred_element_type=F32)
              + ga_ref[...].astype(F32) * jnp.dot(ya.astype(BF16), wba_ref[...].astype(BF16), preferred_element_type=F32))
    out = jnp.dot(merged.astype(BF16), wout_ref[...].astype(BF16), preferred_element_type=F32)
    ms = jnp.mean(out * out, axis=-1, keepdims=True)
    o_ref[...] = x_ref[...] + out * lax.rsqrt(ms + EPS) * g_ref[...]


def _out_block(x, yg, zs, ya, za, gs, ga, wglu, bglu, wbs, wba, wout, g, li, *, tm):
    t = x.shape[0]
    npb = yg.shape[1] // tm
    row = lambda n: pl.BlockSpec((tm, n), lambda i: (i, 0))
    seq = pl.BlockSpec((None, tm, SSM_WIDTH), lambda i: (i // npb, i % npb, 0))
    full = lambda a: _layer_spec(a, li)
    weights = (wglu, bglu, wbs, wba, wout, g)
    return pl.pallas_call(
        _out_kernel,
        grid=(t // tm,),
        in_specs=[row(D_MODEL), seq, row(512),
                  pl.BlockSpec((N_HEAD_PAIRS, tm, LANES), lambda i: (0, i, 0)),
                  row(512), row(D_MODEL), row(D_MODEL)] + [full(a) for a in weights],
        out_specs=row(D_MODEL),
        out_shape=jax.ShapeDtypeStruct(x.shape, F32),
        compiler_params=pltpu.CompilerParams(
            dimension_semantics=("arbitrary",), vmem_limit_bytes=VMEM_LIMIT),
        name="out_block",
    )(x, yg, zs, ya, za, gs, ga, *weights)


def _tile(n, target):
    t = min(n, target)
    assert n % t == 0, (n, target)
    return t


def kernel(x, pre_norm_g, post_norm_g, w_in, ssm_a_re, ssm_a_im, ssm_log_dt, ssm_b_re, ssm_b_im,
           ssm_c_re, ssm_c_im, ssm_d, w_glu, b_glu, w_branch_ssm, w_branch_attn, w_out):
    nb, l, d = x.shape
    depth = w_in.shape[0]
    assert d == D_MODEL and w_in.shape[2] == IN_COLS
    t = nb * l
    tm = _tile(l, 1024)
    lc = _tile(l, 512)
    tb = _tile(l, 256)

    abar_re, abar_im, wb, wc = _ssm_params(
        ssm_a_re, ssm_a_im, ssm_log_dt[..., None],
        ssm_b_re.transpose(0, 1, 3, 2), ssm_b_im.transpose(0, 1, 3, 2),
        ssm_c_re, ssm_c_im)
    ar = abar_re.reshape(depth, 2, SUBLANES, LANES)
    ai = abar_im.reshape(depth, 2, SUBLANES, LANES)

    row = lambda a: a[:, None, :]
    pre_g, post_g, d_skip, b_glu_r = row(pre_norm_g), row(post_norm_g), row(ssm_d), row(b_glu)

    xt = x.reshape(t, d)
    for li in range(depth):
        u, zs, q, k, v, za, gs, ga = _in_proj(xt, pre_g, w_in, li, nb=nb, tm=tm)
        yg = _ssm(u, wb, ar, ai, wc, d_skip, li, lc=lc)
        ya = _attention(q, k, v, nb=nb, tb=tb)
        xt = _out_block(xt, yg, zs, ya, za, gs, ga,
                        w_glu, b_glu_r, w_branch_ssm, w_branch_attn, w_out, post_g, li, tm=tm)
    return xt.reshape(nb, l, d)
```

```python
import functools
import math

import jax
import jax.numpy as jnp
from jax import lax
from jax.experimental import pallas as pl
from jax.experimental.pallas import tpu as pltpu

F32 = jnp.float32
BF16 = jnp.bfloat16

D_MODEL = 1024
SSM_WIDTH = 512
SSM_GROUP = 16
SSM_GROUPS = 32
SSM_STATE = 64
ATTN_WIDTH = 512
HEAD_DIM = 64
EPS = 1e-6
IN_COLS = 5120

LANES = 128
SUBLANES = 8
N_SLABS = SSM_WIDTH // LANES
SLAB_STATE = (LANES // SSM_GROUP) * SSM_STATE
STATE_CHUNKS = SLAB_STATE // LANES
N_STATE_VREGS = 4
SCAN_UNROLL = 32
N_HEAD_PAIRS = ATTN_WIDTH // LANES

VMEM_LIMIT = 56 * 1024 * 1024

LOG2E = math.log2(math.e)
Q_SCALE = HEAD_DIM ** -0.5 * LOG2E
EXIT_LOG2 = 150.0
NO_BLOCK = 1e30


def _sigmoid(x):
    return 1.0 / (1.0 + jnp.exp(-x))


def _silu(x):
    return x * _sigmoid(x)


def _gelu_tanh(x):
    return 0.5 * x * (1.0 + jnp.tanh(math.sqrt(2.0 / math.pi) * (x + 0.044715 * (x * x * x))))


IN_PROJ_ORDER = ((3072, 4096), (4096, 5120), (512, 1024), (2560, 3072), (1024, 1536), (1536, 2048),
                 (2048, 2560), (0, 512))


def _in_proj_kernel(x_ref, g_ref, w_hbm, u_ref, zs_ref, q_ref, k_ref, v_ref, za_ref, gs_ref, ga_ref,
                    w_bf16, stage, w_sem, *, li):
    first_step = pl.program_id(0) == 0
    n_ranges = len(IN_PROJ_ORDER)

    def weight_copy(n):
        lo, hi = IN_PROJ_ORDER[n]
        slot = n % 2
        return pltpu.make_async_copy(w_hbm.at[li, :, lo:hi], stage.at[slot, :, :hi - lo], w_sem.at[slot])

    @pl.when(first_step)
    def _():
        weight_copy(0).start()
        weight_copy(1).start()

    def project(wait):
        x = x_ref[...]
        ms = jnp.mean(x * x, axis=-1, keepdims=True)
        h = (x * lax.rsqrt(ms + EPS) * g_ref[...]).astype(BF16)

        def proj(n):
            lo, hi = IN_PROJ_ORDER[n]
            if wait:
                weight_copy(n).wait()
                w_bf16[:, lo:hi] = stage[n % 2, :, :hi - lo].astype(BF16)
                if n + 2 < n_ranges:
                    weight_copy(n + 2).start()
            return jnp.dot(h, w_bf16[:, lo:hi], preferred_element_type=F32)

        gs_ref[...] = _sigmoid(proj(0)).astype(BF16)
        ga_ref[...] = _sigmoid(proj(1)).astype(BF16)
        zs_ref[...] = _silu(proj(2)).astype(BF16)
        za_ref[...] = _silu(proj(3)).astype(BF16)
        for n, ref, scale in ((4, q_ref, Q_SCALE), (5, k_ref, None), (6, v_ref, None)):
            y = proj(n)
            if scale is not None:
                y = y * scale
            for hp in range(N_HEAD_PAIRS):
                ref[hp] = y[:, hp * LANES:(hp + 1) * LANES].astype(BF16)
        u_ref[...] = proj(7).astype(BF16)

    @pl.when(first_step)
    def _():
        project(wait=True)

    @pl.when(jnp.logical_not(first_step))
    def _():
        project(wait=False)


def _layer_spec(a, li):
    return pl.BlockSpec((None,) + a.shape[1:], lambda *_: (li,) + (0,) * (a.ndim - 1),
                        pipeline_mode=pl.Buffered(1))


def _in_proj(x, g, w, li, *, nb, tm):
    t = x.shape[0]
    npb = t // nb // tm
    row = lambda n: pl.BlockSpec((tm, n), lambda i: (i, 0))
    seq = pl.BlockSpec((None, tm, SSM_WIDTH), lambda i: (i // npb, i % npb, 0))
    full = lambda a: _layer_spec(a, li)
    heads = pl.BlockSpec((N_HEAD_PAIRS, tm, LANES), lambda i: (0, i, 0))
    heads_shape = jax.ShapeDtypeStruct((N_HEAD_PAIRS, t, LANES), BF16)
    flat = lambda n: jax.ShapeDtypeStruct((t, n), BF16)
    return pl.pallas_call(
        functools.partial(_in_proj_kernel, li=li),
        grid=(t // tm,),
        in_specs=[row(D_MODEL), full(g), pl.BlockSpec(memory_space=pl.ANY)],
        out_specs=[seq, row(512), heads, heads, heads, row(512), row(1024), row(1024)],
        out_shape=[jax.ShapeDtypeStruct((nb, t // nb, SSM_WIDTH), BF16), flat(512),
                   heads_shape, heads_shape, heads_shape, flat(512), flat(1024), flat(1024)],
        scratch_shapes=[pltpu.VMEM(w.shape[1:], BF16),
                        pltpu.VMEM((2, w.shape[1], max(hi - lo for lo, hi in IN_PROJ_ORDER)), F32),
                        pltpu.SemaphoreType.DMA((2,))],
        compiler_params=pltpu.CompilerParams(
            dimension_semantics=("arbitrary",), vmem_limit_bytes=VMEM_LIMIT),
        name="in_proj",
    )(x, g, w)


def _ssm_param_kernel(are_ref, aim_ref, ldt_ref, bre_ref, bim_ref, cre_ref, cim_ref,
                      abr_ref, abi_ref, wb_ref, wc_ref):
    a_re = are_ref[0]
    a_im = aim_ref[0]
    dt = jnp.exp(ldt_ref[0])
    mag = jnp.exp(a_re * dt)
    abar_re = mag * jnp.cos(a_im * dt)
    abar_im = mag * jnp.sin(a_im * dt)
    nr = abar_re - 1.0
    ni = abar_im
    den = a_re * a_re + a_im * a_im
    f_re = ((nr * a_re + ni * a_im) / den)[:, None, :]
    f_im = ((ni * a_re - nr * a_im) / den)[:, None, :]
    abr_ref[0] = abar_re
    abi_ref[0] = abar_im
    b_re = bre_ref[0]
    b_im = bim_ref[0]
    bb_re = f_re * b_re - f_im * b_im
    bb_im = f_re * b_im + f_im * b_re

    gps = LANES // SSM_GROUP
    iota = lambda shape, dim: lax.broadcasted_iota(jnp.int32, shape, dim)
    spread = (iota((SSM_STATE, SLAB_STATE), 1) % SSM_STATE == iota((SSM_STATE, SLAB_STATE), 0)).astype(BF16)
    stack = (iota((SLAB_STATE, SSM_STATE), 0) % SSM_STATE == iota((SLAB_STATE, SSM_STATE), 1)).astype(BF16)
    keep_b = iota((LANES, SLAB_STATE), 0) // SSM_GROUP == iota((LANES, SLAB_STATE), 1) // SSM_STATE
    keep_c = iota((SLAB_STATE, LANES), 0) // SSM_STATE == iota((SLAB_STATE, LANES), 1) // SSM_GROUP

    def slab_rows(w, s):
        return w[s * gps:(s + 1) * gps].reshape(LANES, SSM_STATE).astype(BF16)

    for s in range(N_SLABS):
        for part, w in enumerate((bb_re, bb_im)):
            wide = jnp.dot(slab_rows(w, s), spread, preferred_element_type=F32)
            wb_ref[0, s, :, part * SLAB_STATE:(part + 1) * SLAB_STATE] = jnp.where(keep_b, wide, 0.0).astype(BF16)
        for part, (w, sign) in enumerate(((cre_ref[0], 1.0), (cim_ref[0], -1.0))):
            tall = lax.dot_general(stack, slab_rows(w, s), (((1,), (1,)), ((), ())),
                                   preferred_element_type=F32)
            wc_ref[0, s, part * SLAB_STATE:(part + 1) * SLAB_STATE, :] = jnp.where(keep_c, sign * tall, 0.0).astype(BF16)


def _ssm_params(a_re, a_im, log_dt, b_re_t, b_im_t, c_re, c_im):
    depth = a_re.shape[0]
    spec = lambda shape: pl.BlockSpec((1,) + shape[1:], lambda l: (l,) + (0,) * (len(shape) - 1))
    ins = (a_re, a_im, log_dt, b_re_t, b_im_t, c_re, c_im)
    out_shapes = [jax.ShapeDtypeStruct(a_re.shape, F32), jax.ShapeDtypeStruct(a_re.shape, F32),
                  jax.ShapeDtypeStruct((depth, N_SLABS, LANES, 2 * SLAB_STATE), BF16),
                  jax.ShapeDtypeStruct((depth, N_SLABS, 2 * SLAB_STATE, LANES), BF16)]
    return pl.pallas_call(
        _ssm_param_kernel,
        grid=(depth,),
        in_specs=[spec(a.shape) for a in ins],
        out_specs=[spec(o.shape) for o in out_shapes],
        out_shape=out_shapes,
        name="ssm_params",
    )(*ins)


def _ssm_kernel(u_ref, wb_ref, ar_ref, ai_ref, wc_ref, d_ref, y_ref, bu_scr, x_scr, st_scr, *, nb, lc):
    @pl.when(pl.program_id(0) == 0)
    def _():
        st_scr[...] = jnp.zeros_like(st_scr)

    nt = lc // SUBLANES

    for b in range(nb):
        for s in range(N_SLABS):
            bu = jnp.dot(u_ref[b, :, s * LANES:(s + 1) * LANES], wb_ref[s], preferred_element_type=F32)
            vp, base = s // 2, (s % 2) * STATE_CHUNKS
            for c in range(STATE_CHUNKS):
                j = base + c
                re = bu[:, c * LANES:(c + 1) * LANES]
                im = bu[:, SLAB_STATE + c * LANES:SLAB_STATE + (c + 1) * LANES]
                bu_scr[b, 2 * vp, :, j * SUBLANES:(j + 1) * SUBLANES, :] = re.reshape(nt, SUBLANES, LANES)
                bu_scr[b, 2 * vp + 1, :, j * SUBLANES:(j + 1) * SUBLANES, :] = im.reshape(nt, SUBLANES, LANES)

    a_r = [ar_ref[0], ar_ref[1]]
    a_i = [ai_ref[0], ai_ref[1]]

    def steps(it, carry):
        xs = list(carry)
        for t in range(SCAN_UNROLL):
            i, r = it * (SCAN_UNROLL // SUBLANES) + t // SUBLANES, t % SUBLANES
            row = pl.multiple_of(i * (SUBLANES * SUBLANES) + r * SUBLANES, SUBLANES)
            for b in range(nb):
                for vp in range(2):
                    k = (b * 2 + vp) * 2
                    b_r = bu_scr[b, 2 * vp, i, pl.ds(r, SUBLANES, stride=SUBLANES), :]
                    b_i = bu_scr[b, 2 * vp + 1, i, pl.ds(r, SUBLANES, stride=SUBLANES), :]
                    x_r, x_i = xs[k], xs[k + 1]
                    n_r = a_r[vp] * x_r - a_i[vp] * x_i + b_r
                    n_i = a_r[vp] * x_i + a_i[vp] * x_r + b_i
                    x_scr[b, 2 * vp, pl.ds(row, SUBLANES), :] = n_r
                    x_scr[b, 2 * vp + 1, pl.ds(row, SUBLANES), :] = n_i
                    xs[k], xs[k + 1] = n_r, n_i
        return tuple(xs)

    init = tuple(st_scr[b, v] for b in range(nb) for v in range(N_STATE_VREGS))
    final = lax.fori_loop(0, lc // SCAN_UNROLL, steps, init)
    for b in range(nb):
        for v in range(N_STATE_VREGS):
            st_scr[b, v] = final[b * N_STATE_VREGS + v]

    for b in range(nb):
        for s in range(N_SLABS):
            vp, base = s // 2, (s % 2) * STATE_CHUNKS
            parts = [x_scr[b, 2 * vp + ri, pl.ds(base + c, lc, stride=SUBLANES), :]
                     for ri in range(2) for c in range(STATE_CHUNKS)]
            xs = jnp.concatenate(parts, axis=1).astype(BF16)
            y = jnp.dot(xs, wc_ref[s], preferred_element_type=F32)
            sl = slice(s * LANES, (s + 1) * LANES)
            y = y + d_ref[:, sl] * u_ref[b, :, sl].astype(F32)
            y_ref[b, :, sl] = _gelu_tanh(y).astype(BF16)


def _ssm(u, wb, ar, ai, wc, d, li, *, lc):
    nb, l, _ = u.shape
    full = lambda a: _layer_spec(a, li)
    blk = pl.BlockSpec((nb, lc, SSM_WIDTH), lambda c: (0, c, 0))
    return pl.pallas_call(
        functools.partial(_ssm_kernel, nb=nb, lc=lc),
        grid=(l // lc,),
        in_specs=[blk, full(wb), full(ar), full(ai), full(wc), full(d)],
        out_specs=blk,
        out_shape=jax.ShapeDtypeStruct(u.shape, BF16),
        scratch_shapes=[
            pltpu.VMEM((nb, N_STATE_VREGS, lc // SUBLANES, SUBLANES * SUBLANES, LANES), F32),
            pltpu.VMEM((nb, N_STATE_VREGS, lc * SUBLANES, LANES), F32),
            pltpu.VMEM((nb, N_STATE_VREGS, SUBLANES, LANES), F32),
        ],
        compiler_params=pltpu.CompilerParams(
            dimension_semantics=("arbitrary",), vmem_limit_bytes=VMEM_LIMIT),
        name="ssm",
    )(u, wb, ar, ai, wc, d)


def _softplus2(z):
    return jnp.maximum(z, 0.0) + jnp.log2(1.0 + jnp.exp2(-jnp.abs(z)))


def _attn_kernel(q_ref, kd_ref, kp_ref, vd_ref, vp_ref, k_hbm, v_hbm, tri_ref, o_ref,
                 acc_scr, carry_scr, kv_buf, kv_sem, *, tb, seq_len):
    qi = pl.program_id(1)
    low = lax.broadcasted_iota(jnp.int32, (tb, LANES), 1) < HEAD_DIM
    causal = (lax.broadcasted_iota(jnp.int32, (tb, tb), 1)
              < lax.broadcasted_iota(jnp.int32, (tb, tb), 0))
    tri = tri_ref[...]
    no_prev = jnp.where(qi > 0, 0.0, NO_BLOCK)

    causal2 = jnp.concatenate([causal, causal], axis=0)
    half = tb // 2
    first = tb // 4

    def stacked_q(hp):
        q = q_ref[hp]
        zero = jnp.zeros_like(q)
        return jnp.concatenate([jnp.where(low, q, zero), jnp.where(low, zero, q)], axis=0)

    def scores(q2, k):
        return lax.dot_general(q2, k, (((1,), (1,)), ((), ())), preferred_element_type=F32)

    def unstack(x):
        return jnp.where(low, x[:tb], x[tb:])

    def near_prev(f, *tiles, dtype):
        rows = []
        for h in range(2):
            top = slice(h * tb, h * tb + first)
            bot = slice(h * tb + first, (h + 1) * tb)
            rows.append(f(*(t[top, :] for t in tiles)).astype(dtype))
            rows.append(jnp.concatenate([jnp.zeros((tb - first, half), dtype),
                                         f(*(t[bot, half:] for t in tiles)).astype(dtype)], axis=1))
        return jnp.concatenate(rows, axis=0)

    def stage_a(hp):
        q2 = stacked_q(hp)
        z0 = scores(q2, kd_ref[hp])
        z1 = scores(q2, kp_ref[hp])
        sp = jnp.concatenate([jnp.where(causal2, _softplus2(z0), 0.0).astype(BF16),
                              near_prev(_softplus2, z1, dtype=BF16)], axis=0)
        return z0, z1, sp

    def stage_b(hp, z0, z1, sp):
        cum = jnp.dot(sp, tri, preferred_element_type=F32)
        cum0, cum1 = cum[:2 * tb], cum[2 * tb:]
        c0 = cum0[:, 0:1]
        w0 = jnp.where(causal2, jnp.exp2(z0 - cum0), 0.0).astype(BF16)
        c0p = jnp.broadcast_to(c0 + no_prev, (2 * tb, tb))
        w1 = near_prev(lambda z, c, p: jnp.exp2(z - c - p), z1, cum1, c0p, dtype=BF16)
        return w0, w1, c0 + cum1[:, 0:1]

    def stage_c(hp, w0, w1, carry, min_carry):
        out = unstack(jnp.dot(w0, vd_ref[hp], preferred_element_type=F32)
                      + jnp.dot(w1, vp_ref[hp], preferred_element_type=F32))
        carry_scr[hp] = carry
        acc_scr[hp] = out
        o_ref[hp] = out.astype(o_ref.dtype)
        return jnp.minimum(min_carry, jnp.minimum(carry[:tb], carry[tb:]))

    min_carry = jnp.full((tb, 1), jnp.inf, F32)
    a_out, b_out = {}, {}
    for step in range(N_HEAD_PAIRS + 2):
        if step < N_HEAD_PAIRS:
            a_out[step] = stage_a(step)
        if 0 <= step - 1 < N_HEAD_PAIRS:
            b_out[step - 1] = stage_b(step - 1, *a_out.pop(step - 1))
        if 0 <= step - 2 < N_HEAD_PAIRS:
            min_carry = stage_c(step - 2, *b_out.pop(step - 2), min_carry)

    @pl.when(jnp.logical_and(qi >= 1, jnp.min(min_carry) < EXIT_LOG2))
    def _():
        row_in_block = lax.broadcasted_iota(jnp.int32, (2 * tb, tb), 0) % tb
        skipped = jnp.logical_and(row_in_block >= first,
                                  lax.broadcasted_iota(jnp.int32, (2 * tb, tb), 1) < half)

        def far(hp, _):
            q2 = stacked_q(hp)

            z = scores(q2, kp_ref[hp])
            sp = jnp.where(skipped, _softplus2(z), 0.0)
            cum = jnp.dot(sp.astype(BF16), tri, preferred_element_type=F32)
            carry = carry_scr[hp]
            w = jnp.where(skipped, jnp.exp2(z - cum - carry), 0.0)
            acc0 = jnp.dot(w.astype(BF16), vp_ref[hp], preferred_element_type=F32)
            carry0 = carry + cum[:, 0:1]

            def fetch(j):
                rows = pl.ds(pl.multiple_of(pl.program_id(0) * seq_len + j * tb, tb), tb)
                copies = [pltpu.make_async_copy(src.at[hp, rows, :], kv_buf.at[n], kv_sem.at[n])
                          for n, src in enumerate((k_hbm, v_hbm))]
                for c in copies:
                    c.start()
                for c in copies:
                    c.wait()

            def cond(st):
                j, carry, _ = st
                return jnp.logical_and(j >= 0, jnp.min(carry) < EXIT_LOG2)

            def body(st):
                j, carry, acc = st
                fetch(j)
                z = scores(q2, kv_buf[0])
                cum = jnp.dot(_softplus2(z).astype(BF16), tri, preferred_element_type=F32)
                w = jnp.exp2(z - cum - carry).astype(BF16)
                return j - 1, carry + cum[:, 0:1], acc + jnp.dot(w, kv_buf[1], preferred_element_type=F32)

            init = (qi - 2, carry0, acc0)
            acc = lax.while_loop(cond, body, init)[2]
            o_ref[hp] = (acc_scr[hp] + unstack(acc)).astype(o_ref.dtype)
            return 0

        lax.fori_loop(0, N_HEAD_PAIRS, far, 0)


def _attention(q, k, v, *, nb, tb):
    nhp, t, _ = q.shape
    l = t // nb
    nq = l // tb
    tri = (lax.broadcasted_iota(jnp.int32, (tb, tb), 0) >= lax.broadcasted_iota(jnp.int32, (tb, tb), 1)).astype(BF16)
    qspec = pl.BlockSpec((nhp, tb, LANES), lambda b, i: (0, b * nq + i, 0))
    prev = pl.BlockSpec((nhp, tb, LANES), lambda b, i: (0, b * nq + jnp.maximum(i - 1, 0), 0))
    hbm = pl.BlockSpec(memory_space=pl.ANY)
    return pl.pallas_call(
        functools.partial(_attn_kernel, tb=tb, seq_len=l),
        grid=(nb, nq),
        in_specs=[qspec, qspec, prev, qspec, prev, hbm, hbm, pl.BlockSpec((tb, tb), lambda b, i: (0, 0))],
        out_specs=qspec,
        out_shape=jax.ShapeDtypeStruct(q.shape, BF16),
        scratch_shapes=[pltpu.VMEM((nhp, tb, LANES), F32), pltpu.VMEM((nhp, 2 * tb, 1), F32),
                        pltpu.VMEM((2, tb, LANES), BF16), pltpu.SemaphoreType.DMA((2,))],
        compiler_params=pltpu.CompilerParams(
            dimension_semantics=("arbitrary", "arbitrary"), vmem_limit_bytes=VMEM_LIMIT),
        name="attention",
    )(q, k, k, v, v, k, v, tri)


def _out_kernel(x_ref, yg_ref, zs_ref, ya_ref, za_ref, gs_ref, ga_ref,
                wglu_ref, bglu_ref, wbs_ref, wba_ref, wout_ref, g_ref, o_ref):
    gl = jnp.dot(yg_ref[...], wglu_ref[...].astype(BF16), preferred_element_type=F32) + bglu_ref[...]
    ys = gl[:, :SSM_WIDTH] * _sigmoid(gl[:, SSM_WIDTH:]) * zs_ref[...].astype(F32)
    ya = jnp.concatenate([ya_ref[hp] for hp in range(N_HEAD_PAIRS)], axis=1).astype(F32) * za_ref[...].astype(F32)
    merged = (gs_ref[...].astype(F32) * jnp.dot(ys.astype(BF16), wbs_ref[...].astype(BF16), preferred_element_type=F32)
              + ga_ref[...].astype(F32) * jnp.dot(ya.astype(BF16), wba_ref[...].astype(BF16), preferred_element_type=F32))
    out = jnp.dot(merged.astype(BF16), wout_ref[...].astype(BF16), preferred_element_type=F32)
    ms = jnp.mean(out * out, axis=-1, keepdims=True)
    o_ref[...] = x_ref[...] + out * lax.rsqrt(ms + EPS) * g_ref[...]


def _out_block(x, yg, zs, ya, za, gs, ga, wglu, bglu, wbs, wba, wout, g, li, *, tm):
    t = x.shape[0]
    npb = yg.shape[1] // tm
    row = lambda n: pl.BlockSpec((tm, n), lambda i: (i, 0))
    seq = pl.BlockSpec((None, tm, SSM_WIDTH), lambda i: (i // npb, i % npb, 0))
    full = lambda a: _layer_spec(a, li)
    weights = (wglu, bglu, wbs, wba, wout, g)
    return pl.pallas_call(
        _out_kernel,
        grid=(t // tm,),
        in_specs=[row(D_MODEL), seq, row(512),
                  pl.BlockSpec((N_HEAD_PAIRS, tm, LANES), lambda i: (0, i, 0)),
                  row(512), row(D_MODEL), row(D_MODEL)] + [full(a) for a in weights],
        out_specs=row(D_MODEL),
        out_shape=jax.ShapeDtypeStruct(x.shape, F32),
        compiler_params=pltpu.CompilerParams(
            dimension_semantics=("arbitrary",), vmem_limit_bytes=VMEM_LIMIT),
        name="out_block",
    )(x, yg, zs, ya, za, gs, ga, *weights)


def _tile(n, target):
    t = min(n, target)
    assert n % t == 0, (n, target)
    return t


def kernel(x, pre_norm_g, post_norm_g, w_in, ssm_a_re, ssm_a_im, ssm_log_dt, ssm_b_re, ssm_b_im,
           ssm_c_re, ssm_c_im, ssm_d, w_glu, b_glu, w_branch_ssm, w_branch_attn, w_out):
    nb, l, d = x.shape
    depth = w_in.shape[0]
    assert d == D_MODEL and w_in.shape[2] == IN_COLS
    t = nb * l
    tm = _tile(l, 1024)
    lc = _tile(l, 512)
    tb = _tile(l, 256)

    abar_re, abar_im, wb, wc = _ssm_params(
        ssm_a_re, ssm_a_im, ssm_log_dt[..., None],
        ssm_b_re.transpose(0, 1, 3, 2), ssm_b_im.transpose(0, 1, 3, 2),
        ssm_c_re, ssm_c_im)
    ar = abar_re.reshape(depth, 2, SUBLANES, LANES)
    ai = abar_im.reshape(depth, 2, SUBLANES, LANES)

    row = lambda a: a[:, None, :]
    pre_g, post_g, d_skip, b_glu_r = row(pre_norm_g), row(post_norm_g), row(ssm_d), row(b_glu)

    xt = x.reshape(t, d)
    for li in range(depth):
        u, zs, q, k, v, za, gs, ga = _in_proj(xt, pre_g, w_in, li, nb=nb, tm=tm)
        yg = _ssm(u, wb, ar, ai, wc, d_skip, li, lc=lc)
        ya = _attention(q, k, v, nb=nb, tb=tb)
        xt = _out_block(xt, yg, zs, ya, za, gs, ga,
                        w_glu, b_glu_r, w_branch_ssm, w_branch_attn, w_out, post_g, li, tm=tm)
    return xt.reshape(nb, l, d)
```

```python
import functools
import math

import jax
import jax.numpy as jnp
from jax import lax
from jax.experimental import pallas as pl
from jax.experimental.pallas import tpu as pltpu

F32 = jnp.float32
BF16 = jnp.bfloat16

D_MODEL = 1024
SSM_WIDTH = 512
SSM_GROUP = 16
SSM_GROUPS = 32
SSM_STATE = 64
ATTN_WIDTH = 512
HEAD_DIM = 64
EPS = 1e-6
IN_COLS = 5120

LANES = 128
SUBLANES = 8
N_SLABS = SSM_WIDTH // LANES
SLAB_STATE = (LANES // SSM_GROUP) * SSM_STATE
STATE_CHUNKS = SLAB_STATE // LANES
N_STATE_VREGS = 4
SCAN_UNROLL = 32
N_HEAD_PAIRS = ATTN_WIDTH // LANES

VMEM_LIMIT = 56 * 1024 * 1024

LOG2E = math.log2(math.e)
Q_SCALE = HEAD_DIM ** -0.5 * LOG2E
EXIT_LOG2 = 150.0
NO_BLOCK = 1e30


def _sigmoid(x):
    return 1.0 / (1.0 + jnp.exp(-x))


def _silu(x):
    return x * _sigmoid(x)


def _gelu_tanh(x):
    return 0.5 * x * (1.0 + jnp.tanh(math.sqrt(2.0 / math.pi) * (x + 0.044715 * (x * x * x))))


IN_PROJ_ORDER = ((3072, 4096), (4096, 5120), (512, 1024), (2560, 3072), (1024, 1536), (1536, 2048),
                 (2048, 2560), (0, 512))


def _in_proj_kernel(x_ref, g_ref, w_hbm, u_ref, zs_ref, q_ref, k_ref, v_ref, za_ref, gs_ref, ga_ref,
                    w_buf, w_sem, *, li):
    first_step = pl.program_id(0) == 0

    def weight_copy(n):
        lo, hi = IN_PROJ_ORDER[n]
        return pltpu.make_async_copy(w_hbm.at[li, :, lo:hi], w_buf.at[:, lo:hi], w_sem.at[n])

    @pl.when(first_step)
    def _():
        for n in range(len(IN_PROJ_ORDER)):
            weight_copy(n).start(priority=n % 2)

    def project(wait):
        x = x_ref[...]
        ms = jnp.mean(x * x, axis=-1, keepdims=True)
        h = (x * lax.rsqrt(ms + EPS) * g_ref[...]).astype(BF16)

        def proj(n):
            if wait:
                weight_copy(n).wait()
            lo, hi = IN_PROJ_ORDER[n]
            return jnp.dot(h, w_buf[:, lo:hi].astype(BF16), preferred_element_type=F32)

        gs_ref[...] = _sigmoid(proj(0)).astype(BF16)
        ga_ref[...] = _sigmoid(proj(1)).astype(BF16)
        zs_ref[...] = _silu(proj(2)).astype(BF16)
        za_ref[...] = _silu(proj(3)).astype(BF16)
        for n, ref, scale in ((4, q_ref, Q_SCALE), (5, k_ref, None), (6, v_ref, None)):
            y = proj(n)
            if scale is not None:
                y = y * scale
            for hp in range(N_HEAD_PAIRS):
                ref[hp] = y[:, hp * LANES:(hp + 1) * LANES].astype(BF16)
        u_ref[...] = proj(7).astype(BF16)

    @pl.when(first_step)
    def _():
        project(wait=True)

    @pl.when(jnp.logical_not(first_step))
    def _():
        project(wait=False)


def _layer_spec(a, li):
    return pl.BlockSpec((None,) + a.shape[1:], lambda *_: (li,) + (0,) * (a.ndim - 1),
                        pipeline_mode=pl.Buffered(1))


def _in_proj(x, g, w, li, *, nb, tm):
    t = x.shape[0]
    npb = t // nb // tm
    row = lambda n: pl.BlockSpec((tm, n), lambda i: (i, 0))
    seq = pl.BlockSpec((None, tm, SSM_WIDTH), lambda i: (i // npb, i % npb, 0))
    full = lambda a: _layer_spec(a, li)
    heads = pl.BlockSpec((N_HEAD_PAIRS, tm, LANES), lambda i: (0, i, 0))
    heads_shape = jax.ShapeDtypeStruct((N_HEAD_PAIRS, t, LANES), BF16)
    flat = lambda n: jax.ShapeDtypeStruct((t, n), BF16)
    return pl.pallas_call(
        functools.partial(_in_proj_kernel, li=li),
        grid=(t // tm,),
        in_specs=[row(D_MODEL), full(g), pl.BlockSpec(memory_space=pl.ANY)],
        out_specs=[seq, row(512), heads, heads, heads, row(512), row(1024), row(1024)],
        out_shape=[jax.ShapeDtypeStruct((nb, t // nb, SSM_WIDTH), BF16), flat(512),
                   heads_shape, heads_shape, heads_shape, flat(512), flat(1024), flat(1024)],
        scratch_shapes=[pltpu.VMEM(w.shape[1:], F32), pltpu.SemaphoreType.DMA((len(IN_PROJ_ORDER),))],
        compiler_params=pltpu.CompilerParams(
            dimension_semantics=("arbitrary",), vmem_limit_bytes=VMEM_LIMIT),
        name="in_proj",
    )(x, g, w)


def _ssm_param_kernel(are_ref, aim_ref, ldt_ref, bre_ref, bim_ref, cre_ref, cim_ref,
                      abr_ref, abi_ref, wb_ref, wc_ref):
    a_re = are_ref[0]
    a_im = aim_ref[0]
    dt = jnp.exp(ldt_ref[0])
    mag = jnp.exp(a_re * dt)
    abar_re = mag * jnp.cos(a_im * dt)
    abar_im = mag * jnp.sin(a_im * dt)
    nr = abar_re - 1.0
    ni = abar_im
    den = a_re * a_re + a_im * a_im
    f_re = ((nr * a_re + ni * a_im) / den)[:, None, :]
    f_im = ((ni * a_re - nr * a_im) / den)[:, None, :]
    abr_ref[0] = abar_re
    abi_ref[0] = abar_im
    b_re = bre_ref[0]
    b_im = bim_ref[0]
    bb_re = f_re * b_re - f_im * b_im
    bb_im = f_re * b_im + f_im * b_re

    gps = LANES // SSM_GROUP
    iota = lambda shape, dim: lax.broadcasted_iota(jnp.int32, shape, dim)
    spread = (iota((SSM_STATE, SLAB_STATE), 1) % SSM_STATE == iota((SSM_STATE, SLAB_STATE), 0)).astype(BF16)
    stack = (iota((SLAB_STATE, SSM_STATE), 0) % SSM_STATE == iota((SLAB_STATE, SSM_STATE), 1)).astype(BF16)
    keep_b = iota((LANES, SLAB_STATE), 0) // SSM_GROUP == iota((LANES, SLAB_STATE), 1) // SSM_STATE
    keep_c = iota((SLAB_STATE, LANES), 0) // SSM_STATE == iota((SLAB_STATE, LANES), 1) // SSM_GROUP

    def slab_rows(w, s):
        return w[s * gps:(s + 1) * gps].reshape(LANES, SSM_STATE).astype(BF16)

    for s in range(N_SLABS):
        for part, w in enumerate((bb_re, bb_im)):
            wide = jnp.dot(slab_rows(w, s), spread, preferred_element_type=F32)
            wb_ref[0, s, :, part * SLAB_STATE:(part + 1) * SLAB_STATE] = jnp.where(keep_b, wide, 0.0).astype(BF16)
        for part, (w, sign) in enumerate(((cre_ref[0], 1.0), (cim_ref[0], -1.0))):
            tall = lax.dot_general(stack, slab_rows(w, s), (((1,), (1,)), ((), ())),
                                   preferred_element_type=F32)
            wc_ref[0, s, part * SLAB_STATE:(part + 1) * SLAB_STATE, :] = jnp.where(keep_c, sign * tall, 0.0).astype(BF16)


def _ssm_params(a_re, a_im, log_dt, b_re_t, b_im_t, c_re, c_im):
    depth = a_re.shape[0]
    spec = lambda shape: pl.BlockSpec((1,) + shape[1:], lambda l: (l,) + (0,) * (len(shape) - 1))
    ins = (a_re, a_im, log_dt, b_re_t, b_im_t, c_re, c_im)
    out_shapes = [jax.ShapeDtypeStruct(a_re.shape, F32), jax.ShapeDtypeStruct(a_re.shape, F32),
                  jax.ShapeDtypeStruct((depth, N_SLABS, LANES, 2 * SLAB_STATE), BF16),
                  jax.ShapeDtypeStruct((depth, N_SLABS, 2 * SLAB_STATE, LANES), BF16)]
    return pl.pallas_call(
        _ssm_param_kernel,
        grid=(depth,),
        in_specs=[spec(a.shape) for a in ins],
        out_specs=[spec(o.shape) for o in out_shapes],
        out_shape=out_shapes,
        name="ssm_params",
    )(*ins)


def _ssm_kernel(u_ref, wb_ref, ar_ref, ai_ref, wc_ref, d_ref, y_ref, bu_scr, x_scr, st_scr, *, nb, lc):
    @pl.when(pl.program_id(0) == 0)
    def _():
        st_scr[...] = jnp.zeros_like(st_scr)

    nt = lc // SUBLANES

    for b in range(nb):
        for s in range(N_SLABS):
            bu = jnp.dot(u_ref[b, :, s * LANES:(s + 1) * LANES], wb_ref[s], preferred_element_type=F32)
            vp, base = s // 2, (s % 2) * STATE_CHUNKS
            for c in range(STATE_CHUNKS):
                j = base + c
                re = bu[:, c * LANES:(c + 1) * LANES]
                im = bu[:, SLAB_STATE + c * LANES:SLAB_STATE + (c + 1) * LANES]
                bu_scr[b, 2 * vp, :, j * SUBLANES:(j + 1) * SUBLANES, :] = re.reshape(nt, SUBLANES, LANES)
                bu_scr[b, 2 * vp + 1, :, j * SUBLANES:(j + 1) * SUBLANES, :] = im.reshape(nt, SUBLANES, LANES)

    a_r = [ar_ref[0], ar_ref[1]]
    a_i = [ai_ref[0], ai_ref[1]]

    def steps(it, carry):
        xs = list(carry)
        for t in range(SCAN_UNROLL):
            i, r = it * (SCAN_UNROLL // SUBLANES) + t // SUBLANES, t % SUBLANES
            row = pl.multiple_of(i * (SUBLANES * SUBLANES) + r * SUBLANES, SUBLANES)
            for b in range(nb):
                for vp in range(2):
                    k = (b * 2 + vp) * 2
                    b_r = bu_scr[b, 2 * vp, i, pl.ds(r, SUBLANES, stride=SUBLANES), :]
                    b_i = bu_scr[b, 2 * vp + 1, i, pl.ds(r, SUBLANES, stride=SUBLANES), :]
                    x_r, x_i = xs[k], xs[k + 1]
                    n_r = a_r[vp] * x_r - a_i[vp] * x_i + b_r
                    n_i = a_r[vp] * x_i + a_i[vp] * x_r + b_i
                    x_scr[b, 2 * vp, pl.ds(row, SUBLANES), :] = n_r
                    x_scr[b, 2 * vp + 1, pl.ds(row, SUBLANES), :] = n_i
                    xs[k], xs[k + 1] = n_r, n_i
        return tuple(xs)

    init = tuple(st_scr[b, v] for b in range(nb) for v in range(N_STATE_VREGS))
    final = lax.fori_loop(0, lc // SCAN_UNROLL, steps, init)
    for b in range(nb):
        for v in range(N_STATE_VREGS):
            st_scr[b, v] = final[b * N_STATE_VREGS + v]

    for b in range(nb):
        for s in range(N_SLABS):
            vp, base = s // 2, (s % 2) * STATE_CHUNKS
            parts = [x_scr[b, 2 * vp + ri, pl.ds(base + c, lc, stride=SUBLANES), :]
                     for ri in range(2) for c in range(STATE_CHUNKS)]
            xs = jnp.concatenate(parts, axis=1).astype(BF16)
            y = jnp.dot(xs, wc_ref[s], preferred_element_type=F32)
            sl = slice(s * LANES, (s + 1) * LANES)
            y = y + d_ref[:, sl] * u_ref[b, :, sl].astype(F32)
            y_ref[b, :, sl] = _gelu_tanh(y).astype(BF16)


def _ssm(u, wb, ar, ai, wc, d, li, *, lc):
    nb, l, _ = u.shape
    full = lambda a: _layer_spec(a, li)
    blk = pl.BlockSpec((nb, lc, SSM_WIDTH), lambda c: (0, c, 0))
    return pl.pallas_call(
        functools.partial(_ssm_kernel, nb=nb, lc=lc),
        grid=(l // lc,),
        in_specs=[blk, full(wb), full(ar), full(ai), full(wc), full(d)],
        out_specs=blk,
        out_shape=jax.ShapeDtypeStruct(u.shape, BF16),
        scratch_shapes=[
            pltpu.VMEM((nb, N_STATE_VREGS, lc // SUBLANES, SUBLANES * SUBLANES, LANES), F32),
            pltpu.VMEM((nb, N_STATE_VREGS, lc * SUBLANES, LANES), F32),
            pltpu.VMEM((nb, N_STATE_VREGS, SUBLANES, LANES), F32),
        ],
        compiler_params=pltpu.CompilerParams(
            dimension_semantics=("arbitrary",), vmem_limit_bytes=VMEM_LIMIT),
        name="ssm",
    )(u, wb, ar, ai, wc, d)


def _softplus2(z):
    return jnp.maximum(z, 0.0) + jnp.log2(1.0 + jnp.exp2(-jnp.abs(z)))


def _attn_kernel(q_ref, kd_ref, kp_ref, vd_ref, vp_ref, k_hbm, v_hbm, tri_ref, o_ref,
                 acc_scr, carry_scr, kv_buf, kv_sem, *, tb, seq_len):
    qi = pl.program_id(1)
    low = lax.broadcasted_iota(jnp.int32, (tb, LANES), 1) < HEAD_DIM
    causal = (lax.broadcasted_iota(jnp.int32, (tb, tb), 1)
              < lax.broadcasted_iota(jnp.int32, (tb, tb), 0))
    tri = tri_ref[...]
    no_prev = jnp.where(qi > 0, 0.0, NO_BLOCK)

    causal2 = jnp.concatenate([causal, causal], axis=0)
    half = tb // 2
    first = tb // 4

    def stacked_q(hp):
        q = q_ref[hp]
        zero = jnp.zeros_like(q)
        return jnp.concatenate([jnp.where(low, q, zero), jnp.where(low, zero, q)], axis=0)

    def scores(q2, k):
        return lax.dot_general(q2, k, (((1,), (1,)), ((), ())), preferred_element_type=F32)

    def unstack(x):
        return jnp.where(low, x[:tb], x[tb:])

    def near_prev(f, *tiles, dtype):
        rows = []
        for h in range(2):
            top = slice(h * tb, h * tb + first)
            bot = slice(h * tb + first, (h + 1) * tb)
            rows.append(f(*(t[top, :] for t in tiles)).astype(dtype))
            rows.append(jnp.concatenate([jnp.zeros((tb - first, half), dtype),
                                         f(*(t[bot, half:] for t in tiles)).astype(dtype)], axis=1))
        return jnp.concatenate(rows, axis=0)

    def stage_a(hp):
        q2 = stacked_q(hp)
        z0 = scores(q2, kd_ref[hp])
        z1 = scores(q2, kp_ref[hp])
        sp = jnp.concatenate([jnp.where(causal2, _softplus2(z0), 0.0).astype(BF16),
                              near_prev(_softplus2, z1, dtype=BF16)], axis=0)
        return z0, z1, sp

    def stage_b(hp, z0, z1, sp):
        cum = jnp.dot(sp, tri, preferred_element_type=F32)
        cum0, cum1 = cum[:2 * tb], cum[2 * tb:]
        c0 = cum0[:, 0:1]
        w0 = jnp.where(causal2, jnp.exp2(z0 - cum0), 0.0).astype(BF16)
        c0p = jnp.broadcast_to(c0 + no_prev, (2 * tb, tb))
        w1 = near_prev(lambda z, c, p: jnp.exp2(z - c - p), z1, cum1, c0p, dtype=BF16)
        return w0, w1, c0 + cum1[:, 0:1]

    def stage_c(hp, w0, w1, carry, min_carry):
        out = unstack(jnp.dot(w0, vd_ref[hp], preferred_element_type=F32)
                      + jnp.dot(w1, vp_ref[hp], preferred_element_type=F32))
        carry_scr[hp] = carry
        acc_scr[hp] = out
        o_ref[hp] = out.astype(o_ref.dtype)
        return jnp.minimum(min_carry, jnp.minimum(carry[:tb], carry[tb:]))

    min_carry = jnp.full((tb, 1), jnp.inf, F32)
    a_out, b_out = {}, {}
    for step in range(N_HEAD_PAIRS + 2):
        if step < N_HEAD_PAIRS:
            a_out[step] = stage_a(step)
        if 0 <= step - 1 < N_HEAD_PAIRS:
            b_out[step - 1] = stage_b(step - 1, *a_out.pop(step - 1))
        if 0 <= step - 2 < N_HEAD_PAIRS:
            min_carry = stage_c(step - 2, *b_out.pop(step - 2), min_carry)

    @pl.when(jnp.logical_and(qi >= 1, jnp.min(min_carry) < EXIT_LOG2))
    def _():
        row_in_block = lax.broadcasted_iota(jnp.int32, (2 * tb, tb), 0) % tb
        skipped = jnp.logical_and(row_in_block >= first,
                                  lax.broadcasted_iota(jnp.int32, (2 * tb, tb), 1) < half)

        def far(hp, _):
            q2 = stacked_q(hp)

            z = scores(q2, kp_ref[hp])
            sp = jnp.where(skipped, _softplus2(z), 0.0)
            cum = jnp.dot(sp.astype(BF16), tri, preferred_element_type=F32)
            carry = carry_scr[hp]
            w = jnp.where(skipped, jnp.exp2(z - cum - carry), 0.0)
            acc0 = jnp.dot(w.astype(BF16), vp_ref[hp], preferred_element_type=F32)
            carry0 = carry + cum[:, 0:1]

            def fetch(j):
                rows = pl.ds(pl.multiple_of(pl.program_id(0) * seq_len + j * tb, tb), tb)
                copies = [pltpu.make_async_copy(src.at[hp, rows, :], kv_buf.at[n], kv_sem.at[n])
                          for n, src in enumerate((k_hbm, v_hbm))]
                for c in copies:
                    c.start()
                for c in copies:
                    c.wait()

            def cond(st):
                j, carry, _ = st
                return jnp.logical_and(j >= 0, jnp.min(carry) < EXIT_LOG2)

            def body(st):
                j, carry, acc = st
                fetch(j)
                z = scores(q2, kv_buf[0])
                cum = jnp.dot(_softplus2(z).astype(BF16), tri, preferred_element_type=F32)
                w = jnp.exp2(z - cum - carry).astype(BF16)
                return j - 1, carry + cum[:, 0:1], acc + jnp.dot(w, kv_buf[1], preferred_element_type=F32)

            init = (qi - 2, carry0, acc0)
            acc = lax.while_loop(cond, body, init)[2]
            o_ref[hp] = (acc_scr[hp] + unstack(acc)).astype(o_ref.dtype)
            return 0

        lax.fori_loop(0, N_HEAD_PAIRS, far, 0)


def _attention(q, k, v, *, nb, tb):
    nhp, t, _ = q.shape
    l = t // nb
    nq = l // tb
    tri = (lax.broadcasted_iota(jnp.int32, (tb, tb), 0) >= lax.broadcasted_iota(jnp.int32, (tb, tb), 1)).astype(BF16)
    qspec = pl.BlockSpec((nhp, tb, LANES), lambda b, i: (0, b * nq + i, 0))
    prev = pl.BlockSpec((nhp, tb, LANES), lambda b, i: (0, b * nq + jnp.maximum(i - 1, 0), 0))
    hbm = pl.BlockSpec(memory_space=pl.ANY)
    return pl.pallas_call(
        functools.partial(_attn_kernel, tb=tb, seq_len=l),
        grid=(nb, nq),
        in_specs=[qspec, qspec, prev, qspec, prev, hbm, hbm, pl.BlockSpec((tb, tb), lambda b, i: (0, 0))],
        out_specs=qspec,
        out_shape=jax.ShapeDtypeStruct(q.shape, BF16),
        scratch_shapes=[pltpu.VMEM((nhp, tb, LANES), F32), pltpu.VMEM((nhp, 2 * tb, 1), F32),
                        pltpu.VMEM((2, tb, LANES), BF16), pltpu.SemaphoreType.DMA((2,))],
        compiler_params=pltpu.CompilerParams(
            dimension_semantics=("arbitrary", "arbitrary"), vmem_limit_bytes=VMEM_LIMIT),
        name="attention",
    )(q, k, k, v, v, k, v, tri)


def _out_kernel(x_ref, yg_ref, zs_ref, ya_ref, za_ref, gs_ref, ga_ref,
                wglu_ref, bglu_ref, wbs_ref, wba_ref, wout_ref, g_ref, o_ref):
    gl = jnp.dot(yg_ref[...], wglu_ref[...].astype(BF16), preferred_element_type=F32) + bglu_ref[...]
    ys = gl[:, :SSM_WIDTH] * _sigmoid(gl[:, SSM_WIDTH:]) * zs_ref[...].astype(F32)
    ya = jnp.concatenate([ya_ref[hp] for hp in range(N_HEAD_PAIRS)], axis=1).astype(F32) * za_ref[...].astype(F32)
    merged = (gs_ref[...].astype(F32) * jnp.dot(ys.astype(BF16), wbs_ref[...].astype(BF16), preferred_element_type=F32)
              + ga_ref[...].astype(F32) * jnp.dot(ya.astype(BF16), wba_ref[...].astype(BF16), preferred_element_type=F32))
    out = jnp.dot(merged.astype(BF16), wout_ref[...].astype(BF16), preferred_element_type=F32)
    ms = jnp.mean(out * out, axis=-1, keepdims=True)
    o_ref[...] = x_ref[...] + out * lax.rsqrt(ms + EPS) * g_ref[...]


def _out_block(x, yg, zs, ya, za, gs, ga, wglu, bglu, wbs, wba, wout, g, li, *, tm):
    t = x.shape[0]
    npb = yg.shape[1] // tm
    row = lambda n: pl.BlockSpec((tm, n), lambda i: (i, 0))
    seq = pl.BlockSpec((None, tm, SSM_WIDTH), lambda i: (i // npb, i % npb, 0))
    full = lambda a: _layer_spec(a, li)
    weights = (wglu, bglu, wbs, wba, wout, g)
    return pl.pallas_call(
        _out_kernel,
        grid=(t // tm,),
        in_specs=[row(D_MODEL), seq, row(512),
                  pl.BlockSpec((N_HEAD_PAIRS, tm, LANES), lambda i: (0, i, 0)),
                  row(512), row(D_MODEL), row(D_MODEL)] + [full(a) for a in weights],
        out_specs=row(D_MODEL),
        out_shape=jax.ShapeDtypeStruct(x.shape, F32),
        compiler_params=pltpu.CompilerParams(
            dimension_semantics=("arbitrary",), vmem_limit_bytes=VMEM_LIMIT),
        name="out_block",
    )(x, yg, zs, ya, za, gs, ga, *weights)


def _tile(n, target):
    t = min(n, target)
    assert n % t == 0, (n, target)
    return t


def kernel(x, pre_norm_g, post_norm_g, w_in, ssm_a_re, ssm_a_im, ssm_log_dt, ssm_b_re, ssm_b_im,
           ssm_c_re, ssm_c_im, ssm_d, w_glu, b_glu, w_branch_ssm, w_branch_attn, w_out):
    nb, l, d = x.shape
    depth = w_in.shape[0]
    assert d == D_MODEL and w_in.shape[2] == IN_COLS
    t = nb * l
    tm = _tile(l, 1024)
    lc = _tile(l, 512)
    tb = _tile(l, 256)

    abar_re, abar_im, wb, wc = _ssm_params(
        ssm_a_re, ssm_a_im, ssm_log_dt[..., None],
        ssm_b_re.transpose(0, 1, 3, 2), ssm_b_im.transpose(0, 1, 3, 2),
        ssm_c_re, ssm_c_im)
    ar = abar_re.reshape(depth, 2, SUBLANES, LANES)
    ai = abar_im.reshape(depth, 2, SUBLANES, LANES)

    row = lambda a: a[:, None, :]
    pre_g, post_g, d_skip, b_glu_r = row(pre_norm_g), row(post_norm_g), row(ssm_d), row(b_glu)

    xt = x.reshape(t, d)
    for li in range(depth):
        u, zs, q, k, v, za, gs, ga = _in_proj(xt, pre_g, w_in, li, nb=nb, tm=tm)
        yg = _ssm(u, wb, ar, ai, wc, d_skip, li, lc=lc)
        ya = _attention(q, k, v, nb=nb, tb=tb)
        xt = _out_block(xt, yg, zs, ya, za, gs, ga,
                        w_glu, b_glu_r, w_branch_ssm, w_branch_attn, w_out, post_g, li, tm=tm)
    return xt.reshape(nb, l, d)
```
